```python
import functools
import jax, jax.numpy as jnp
from jax import lax
import numpy as np

D_MODEL = 1024
BATCH = 2
SEQ = 8192
DEPTH = 1
DEC_BATCH = 32
DEC_SEQ = 8
PAST_LEN = 16384
PAGE_SIZE = 128

N_HEADS_SB = 8
HEAD_DIM = 64
D_SB = N_HEADS_SB * HEAD_DIM
D_CONV = D_MODEL - D_SB
D_IN = 3 * D_SB + 2 * D_CONV
CONV_W = 31
FFN_CONV_W = 3
D_FF = 2816
Q_BLOCK = 128
LN_EPS = 1e-5
ALPHA = (2.0 * DEPTH) ** 0.25
BETA_INIT = (8.0 * DEPTH) ** -0.25
SB_BIAS_HI = -1.0
SB_BIAS_LO = -8.0

kernel_name = 'stickbreak_conformer_hybrid_step'


def _layernorm(x, g, b):
    xf = x.astype(jnp.float32)
    mu = jnp.mean(xf, axis=-1, keepdims=True)
    var = jnp.mean(jnp.square(xf - mu), axis=-1, keepdims=True)
    return ((xf - mu) * lax.rsqrt(var + LN_EPS) * g + b).astype(x.dtype)


def _rmsnorm(x, g):
    xf = x.astype(jnp.float32)
    ms = jnp.mean(jnp.square(xf), axis=-1, keepdims=True)
    return (xf * lax.rsqrt(ms + LN_EPS) * g).astype(x.dtype)


def _causal_dwconv(u, prev, w, b):
    xp = jnp.concatenate([prev.astype(u.dtype), u], axis=1)
    out = lax.conv_general_dilated(
        xp, w[:, None, :].astype(u.dtype), window_strides=(1,), padding='VALID',
        dimension_numbers=('NWC', 'WIO', 'NWC'), feature_group_count=u.shape[-1])
    return out + b, xp[:, -(w.shape[0] - 1):]


def _sb_block(q, k, v, bias, q_pos, k_pos):
    z = jnp.einsum('bqhd,bkhd->bhqk', q.astype(jnp.float32),
                   k.astype(jnp.float32)) * (HEAD_DIM ** -0.5)
    z = z + bias.astype(jnp.float32)[None, :, None, None]
    mask = k_pos[None, :] < q_pos[:, None]
    log_1mb = jnp.where(mask, jax.nn.log_sigmoid(-z), 0.0)
    after = lax.cumsum(log_1mb, axis=3, reverse=True) - log_1mb
    a = jnp.where(mask, jnp.exp(jax.nn.log_sigmoid(z) + after), 0.0)
    return jnp.einsum('bhqk,bkhd->bqhd', a, v.astype(jnp.float32))


def _attend_prompt(q, k, v, bias):
    B, T, H, Dh = q.shape
    nb = T // Q_BLOCK
    qb = q.reshape(B, nb, Q_BLOCK, H, Dh).swapaxes(0, 1)
    q_pos = jnp.arange(T).reshape(nb, Q_BLOCK)
    k_pos = jnp.arange(T)
    o = lax.map(lambda args: _sb_block(args[0], k, v, bias, args[1], k_pos), (qb, q_pos))
    return o.swapaxes(0, 1).reshape(B, T, H, Dh)


def _attend_with_past(q, k, v, bias, k_past, v_past):
    past = k_past.shape[1]
    T = q.shape[1]
    kk = jnp.concatenate([k_past.astype(k.dtype), k], axis=1)
    vv = jnp.concatenate([v_past.astype(v.dtype), v], axis=1)
    q_pos = past + jnp.arange(T)
    k_pos = jnp.arange(past + T)
    return _sb_block(q, kk, vv, bias, q_pos, k_pos)


def _decoder_layer(x, c, conv_prev, ffn_prev, attend, w_ada, b_ada, w_in, sb_bias, w_dw,
                   b_dw, cln_g, cln_b, gn_sb, gn_conv, w_out, ln1_g, ln1_b, w_up, w_fdw,
                   b_fdw, w_down, ln2_g, ln2_b):
    B, T, _ = x.shape
    mod = jax.nn.silu(c) @ w_ada + b_ada
    sh1, sc1, g1, sh2, sc2, g2 = jnp.split(mod[:, None, :], 6, axis=-1)
    h = x * (1 + sc1) + sh1
    proj = h @ w_in
    q, k, v, a, g = jnp.split(
        proj, [D_SB, 2 * D_SB, 3 * D_SB, 3 * D_SB + D_CONV], axis=-1)
    q = q.reshape(B, T, N_HEADS_SB, HEAD_DIM)
    k = k.reshape(B, T, N_HEADS_SB, HEAD_DIM)
    v = v.reshape(B, T, N_HEADS_SB, HEAD_DIM)
    attn = attend(q, k, v, sb_bias).astype(x.dtype)
    attn = _rmsnorm(attn, gn_sb.reshape(N_HEADS_SB, HEAD_DIM)).reshape(B, T, D_SB)
    u = a * jax.nn.sigmoid(g)
    cv, conv_state = _causal_dwconv(u, conv_prev, w_dw, b_dw)
    cv = jax.nn.silu(_layernorm(cv, cln_g, cln_b))
    mixed = jnp.concatenate([attn, _rmsnorm(cv, gn_conv)], axis=-1) @ w_out
    x = _layernorm(ALPHA * x + (1 + g1) * mixed, ln1_g, ln1_b)
    h2 = x * (1 + sc2) + sh2
    val, gt = jnp.split(h2 @ w_up, 2, axis=-1)
    gt_c, ffn_state = _causal_dwconv(gt, ffn_prev, w_fdw, b_fdw)
    f = (jax.nn.gelu(gt_c, approximate=False) * val) @ w_down
    x = _layernorm(ALPHA * x + (1 + g2) * f, ln2_g, ln2_b)
    return x, k, v, conv_state, ffn_state


def setup_inputs(seed: int = 0) -> dict:
    key = jax.random.key(seed)
    ks = jax.random.split(key, 32)
    f32 = jnp.float32
    n_pages = PAST_LEN // PAGE_SIZE
    n_used = DEC_BATCH * n_pages
    n_pool = n_used + max(1, n_used // 4)

    def nrm(k, shape, s):
        return jax.random.normal(k, shape, f32) * s

    def gain(k, n):
        return 1.0 + 0.02 * jax.random.normal(k, (DEPTH, n), f32)

    def bias(k, n):
        return 0.01 * jax.random.normal(k, (DEPTH, n), f32)

    w_in = nrm(ks[10], (DEPTH, D_MODEL, D_IN), D_MODEL ** -0.5)
    w_in = w_in.at[:, :, 2 * D_SB:3 * D_SB].multiply(BETA_INIT)
    sb_bias = (jnp.linspace(SB_BIAS_HI, SB_BIAS_LO, N_HEADS_SB, dtype=f32)[None, :]
               + 0.05 * jax.random.normal(ks[27], (DEPTH, N_HEADS_SB), f32))
    page_table = jax.random.permutation(ks[6], n_pool)[:n_used].reshape(
        DEC_BATCH, n_pages).astype(jnp.int32)
    return {
        'x_prompt': nrm(ks[0], (BATCH, SEQ, D_MODEL), 1.0),
        'x_sample': nrm(ks[1], (DEC_BATCH, DEC_SEQ, D_MODEL), 1.0),
        'cache_k': nrm(ks[2], (DEPTH, n_pool, PAGE_SIZE, N_HEADS_SB, HEAD_DIM), 1.0),
        'cache_v': nrm(ks[3], (DEPTH, n_pool, PAGE_SIZE, N_HEADS_SB, HEAD_DIM), 1.0),
        'state_conv': nrm(ks[4], (DEPTH, DEC_BATCH, CONV_W - 1, D_CONV), 0.5),
        'state_ffn': nrm(ks[5], (DEPTH, DEC_BATCH, FFN_CONV_W - 1, D_FF), 1.0),
        'page_table': page_table,
        'c_prompt': nrm(ks[7], (BATCH, D_MODEL), 1.0),
        'c_sample': nrm(ks[8], (DEC_BATCH, D_MODEL), 1.0),
        'w_ada': nrm(ks[9], (DEPTH, D_MODEL, 6 * D_MODEL), 0.2 * D_MODEL ** -0.5),
        'b_ada': bias(ks[11], 6 * D_MODEL),
        'w_in': w_in,
        'sb_bias': sb_bias,
        'w_dw': nrm(ks[12], (DEPTH, CONV_W, D_CONV), CONV_W ** -0.5),
        'b_dw': bias(ks[13], D_CONV),
        'cln_g': gain(ks[14], D_CONV),
        'cln_b': bias(ks[15], D_CONV),
        'gn_sb': gain(ks[16], D_SB),
        'gn_conv': gain(ks[17], D_CONV),
        'w_out': nrm(ks[18], (DEPTH, D_MODEL, D_MODEL), BETA_INIT * D_MODEL ** -0.5),
        'ln1_g': gain(ks[19], D_MODEL),
        'ln1_b': bias(ks[20], D_MODEL),
        'w_up': nrm(ks[21], (DEPTH, D_MODEL, 2 * D_FF), D_MODEL ** -0.5),
        'w_fdw': nrm(ks[22], (DEPTH, FFN_CONV_W, D_FF), FFN_CONV_W ** -0.5),
        'b_fdw': bias(ks[23], D_FF),
        'w_down': nrm(ks[24], (DEPTH, D_FF, D_MODEL), BETA_INIT * D_FF ** -0.5),
        'ln2_g': gain(ks[25], D_MODEL),
        'ln2_b': bias(ks[26], D_MODEL),
    }


def reference(x_prompt, x_sample, cache_k, cache_v, state_conv, state_ffn, page_table,
              c_prompt, c_sample, w_ada, b_ada, w_in, sb_bias, w_dw, b_dw, cln_g, cln_b,
              gn_sb, gn_conv, w_out, ln1_g, ln1_b, w_up, w_fdw, b_fdw, w_down, ln2_g,
              ln2_b):
    n_seq, n_pages = page_table.shape
    past_len = n_pages * PAGE_SIZE
    xp, xs = x_prompt, x_sample
    kp_l, vp_l, cp_l, fp_l, ks_l, vs_l, cs_l, fs_l = [], [], [], [], [], [], [], []
    for layer in range(DEPTH):
        params = (w_ada[layer], b_ada[layer], w_in[layer], sb_bias[layer], w_dw[layer],
                  b_dw[layer], cln_g[layer], cln_b[layer], gn_sb[layer], gn_conv[layer],
                  w_out[layer], ln1_g[layer], ln1_b[layer], w_up[layer], w_fdw[layer],
                  b_fdw[layer], w_down[layer], ln2_g[layer], ln2_b[layer])
        conv0 = jnp.zeros((xp.shape[0], CONV_W - 1, D_CONV), xp.dtype)
        ffn0 = jnp.zeros((xp.shape[0], FFN_CONV_W - 1, D_FF), xp.dtype)
        xp, kp, vp, cp, fp = _decoder_layer(xp, c_prompt, conv0, ffn0, _attend_prompt, *params)
        k_past = cache_k[layer, page_table].reshape(n_seq, past_len, N_HEADS_SB, HEAD_DIM)
        v_past = cache_v[layer, page_table].reshape(n_seq, past_len, N_HEADS_SB, HEAD_DIM)
        attend_s = functools.partial(_attend_with_past, k_past=k_past, v_past=v_past)
        xs, ks_, vs_, cs_, fs_ = _decoder_layer(xs, c_sample, state_conv[layer],
                                                state_ffn[layer], attend_s, *params)
        kp_l.append(kp); vp_l.append(vp); cp_l.append(cp); fp_l.append(fp)
        ks_l.append(ks_); vs_l.append(vs_); cs_l.append(cs_); fs_l.append(fs_)
    k_prompt = jnp.stack(kp_l)
    v_prompt = jnp.stack(vp_l)
    conv_prompt = jnp.stack(cp_l)
    ffn_prompt = jnp.stack(fp_l)
    k_sample = jnp.stack(ks_l)
    v_sample = jnp.stack(vs_l)
    conv_sample = jnp.stack(cs_l)
    ffn_sample = jnp.stack(fs_l)
    return (xp, xs, k_prompt, v_prompt, conv_prompt, ffn_prompt,
            k_sample, v_sample, conv_sample, ffn_sample)
```

```python
import functools

import jax
import jax.numpy as jnp
from jax import lax
from jax.experimental import pallas as pl
from jax.experimental.pallas import tpu as pltpu

N_HEADS = 8
HEAD_DIM = 64
D_SB = N_HEADS * HEAD_DIM
CONV_W = 31
FFN_CONV_W = 3
PAGE = 128
LN_EPS = 1e-5

LANES = 128
SUBLANES = 8
KEY_BLOCK = LANES
RUN = KEY_BLOCK // SUBLANES
VMEM_LIMIT = 56 * 1024 * 1024

F32 = jnp.float32
BF16 = jnp.bfloat16


def _layernorm(x, g, b):
    mu = jnp.mean(x, axis=-1, keepdims=True)
    xc = x - mu
    var = jnp.mean(xc * xc, axis=-1, keepdims=True)
    return xc * lax.rsqrt(var + LN_EPS) * g + b


def _rms_scale(x, axis):
    ms = jnp.mean(x * x, axis=axis, keepdims=True)
    return x * lax.rsqrt(ms + LN_EPS)


def _log_sigmoid_pair(z):
    sp = jnp.log(1.0 + jnp.exp(-jnp.abs(z)))
    ls = jnp.minimum(z, 0.0) - sp
    return ls, ls - z


def _ada_kernel(c_ref, w_ref, b_ref, o_ref):
    c = c_ref[...]
    s = c * jax.nn.sigmoid(c)
    o_ref[...] = jnp.dot(s, w_ref[...], preferred_element_type=F32) + b_ref[...]


def _ada(c, w, b):
    n, d = c.shape
    n_out = w.shape[1]
    tn = 1536
    return pl.pallas_call(
        _ada_kernel,
        grid=(n_out // tn,),
        in_specs=[pl.BlockSpec((n, d), lambda j: (0, 0)),
                  pl.BlockSpec((d, tn), lambda j: (0, j)),
                  pl.BlockSpec((1, tn), lambda j: (0, j))],
        out_specs=pl.BlockSpec((n, tn), lambda j: (0, j)),
        out_shape=jax.ShapeDtypeStruct((n, n_out), F32),
        compiler_params=pltpu.CompilerParams(dimension_semantics=("arbitrary",),
                                             vmem_limit_bytes=VMEM_LIMIT),
        name="ada",
    )(c, w, b.reshape(1, n_out))


def _inproj_kernel(q_scale, x_ref, sh_ref, sc_ref, w_ref, q_ref, k_ref, v_ref, u_ref):
    h = x_ref[0] * (1.0 + sc_ref[0]) + sh_ref[0]
    proj = jnp.dot(h.astype(BF16), w_ref[...], preferred_element_type=F32)
    q_ref[0] = (proj[:, :D_SB] * q_scale).astype(BF16)
    k_ref[0] = proj[:, D_SB:2 * D_SB]
    v_ref[0] = proj[:, 2 * D_SB:3 * D_SB]
    d_conv = (proj.shape[1] - 3 * D_SB) // 2
    a = proj[:, 3 * D_SB:3 * D_SB + d_conv]
    g = proj[:, 3 * D_SB + d_conv:]
    u_ref[0] = a * jax.nn.sigmoid(g)


def _mod_spec(m):
    return pl.BlockSpec((1,) + m.shape[1:], lambda b, i: (b, 0, 0))


def _const_spec(a):
    nd = a.ndim
    return pl.BlockSpec(a.shape, lambda b, i: (0,) * nd)


def _inproj(x, sh, sc, w_in_bf, tm):
    nb, t, d = x.shape
    d_conv = (w_in_bf.shape[1] - 3 * D_SB) // 2
    row = lambda w: pl.BlockSpec((1, tm, w), lambda b, i: (b, i, 0))
    return pl.pallas_call(
        functools.partial(_inproj_kernel, HEAD_DIM ** -0.5),
        grid=(nb, t // tm),
        in_specs=[row(d), _mod_spec(sh), _mod_spec(sc), _const_spec(w_in_bf)],
        out_specs=[row(D_SB), row(D_SB), row(D_SB), row(d_conv)],
        out_shape=[jax.ShapeDtypeStruct((nb, t, D_SB), BF16),
                   jax.ShapeDtypeStruct((nb, t, D_SB), F32),
                   jax.ShapeDtypeStruct((nb, t, D_SB), F32),
                   jax.ShapeDtypeStruct((nb, t, d_conv), F32)],
        compiler_params=pltpu.CompilerParams(dimension_semantics=("arbitrary", "arbitrary"),
                                             vmem_limit_bytes=VMEM_LIMIT),
        name="inproj",
    )(x, sh, sc, w_in_bf)


def _pattn_kernel(tq, bias_ref, qt_ref, kp_ref, vt_ref, o_ref):
    p = pl.program_id(1)
    i = pl.program_id(2)
    nq = tq // KEY_BLOCK
    kb0 = i * nq
    r = lax.broadcasted_iota(jnp.int32, (KEY_BLOCK, tq), 0)
    key_in_block = (r & (SUBLANES - 1)) * RUN + (r >> 3)
    q_in_block = lax.broadcasted_iota(jnp.int32, (KEY_BLOCK, tq), 1)
    qt = qt_ref[0, 0]
    n_vreg_rows = KEY_BLOCK // SUBLANES

    outs = []
    for h in range(2):
        bias = bias_ref[2 * p + h]
        in_head = (r >= HEAD_DIM * h) & (r < HEAD_DIM * (h + 1))
        qm = jnp.where(in_head, qt, jnp.zeros_like(qt))

        def block(kb, carry, acc, key_off):
            z = jnp.dot(kp_ref[0, 0, kb], qm, preferred_element_type=F32) + bias
            ls, l1m = _log_sigmoid_pair(z)
            if key_off is not None:
                mask = (key_in_block + key_off) < q_in_block
                l1m = jnp.where(mask, l1m, 0.0)
            l3 = l1m.reshape(n_vreg_rows, SUBLANES, tq)
            ls3 = ls.reshape(n_vreg_rows, SUBLANES, tq)
            w = [None] * n_vreg_rows
            run = jnp.zeros((SUBLANES, tq), F32)
            for ii in range(n_vreg_rows - 1, -1, -1):
                w[ii] = ls3[ii] + run
                run = run + l3[ii]
            rows = [None] * SUBLANES
            tot = carry
            for s in range(SUBLANES - 1, -1, -1):
                rows[s] = tot
                tot = tot + run[s:s + 1]
            off = jnp.concatenate(rows, axis=0)
            a = jnp.stack([jnp.exp(w[ii] + off) for ii in range(n_vreg_rows)], axis=0)
            a = a.reshape(KEY_BLOCK, tq)
            if key_off is not None:
                a = jnp.where(mask, a, 0.0)
            vt = vt_ref[0, 0, kb, HEAD_DIM * h:HEAD_DIM * (h + 1), :]
            acc = acc + jnp.dot(vt, a.astype(BF16), preferred_element_type=F32)
            return tot, acc

        carry = jnp.zeros((1, tq), F32)
        acc = jnp.zeros((HEAD_DIM, tq), F32)
        for d in range(nq - 1, -1, -1):
            carry, acc = block(kb0 + d, carry, acc, d * KEY_BLOCK)

        def body(t, c):
            return block(kb0 - 1 - t, c[0], c[1], None)

        carry, acc = lax.fori_loop(0, kb0, body, (carry, acc))
        outs.append(_rms_scale(acc, axis=0))
    o_ref[0] = jnp.concatenate(outs, axis=0).T


def _prompt_attention(q, k, v, bias, tq):
    nb, t, _ = q.shape
    nkb = t // KEY_BLOCK
    n_pairs = D_SB // LANES
    qt = q.reshape(nb, t, n_pairs, LANES).transpose(0, 2, 3, 1)
    kb_ = k.astype(BF16).reshape(nb, nkb, SUBLANES, RUN, n_pairs, LANES)
    kp = kb_.transpose(0, 4, 1, 3, 2, 5).reshape(nb, n_pairs, nkb, KEY_BLOCK, LANES)
    vb_ = v.astype(BF16).reshape(nb, nkb, SUBLANES, RUN, n_pairs, LANES)
    vt = vb_.transpose(0, 4, 1, 5, 3, 2).reshape(nb, n_pairs, nkb, LANES, KEY_BLOCK)
    kv_spec = pl.BlockSpec((1, 1, nkb, KEY_BLOCK, LANES), lambda b, p, i: (b, p, 0, 0, 0))
    return pl.pallas_call(
        functools.partial(_pattn_kernel, tq),
        grid=(nb, n_pairs, t // tq),
        in_specs=[pl.BlockSpec(memory_space=pltpu.SMEM),
                  pl.BlockSpec((1, 1, LANES, tq), lambda b, p, i: (b, p, 0, i)),
                  kv_spec, kv_spec],
        out_specs=pl.BlockSpec((1, tq, LANES), lambda b, p, i: (b, i, p)),
        out_shape=jax.ShapeDtypeStruct((nb, t, D_SB), F32),
        compiler_params=pltpu.CompilerParams(
            dimension_semantics=("arbitrary", "arbitrary", "arbitrary"),
            vmem_limit_bytes=VMEM_LIMIT),
        name="prompt_attn",
    )(bias, qt, kp, vt)


def _sattn_kernel(n_pg, t_new, pt_ref, bias_ref, q_ref, kn_ref, vn_ref, tri_ref, *refs):
    k_refs = refs[:n_pg]
    v_refs = refs[n_pg:2 * n_pg]
    o_ref, carry_ref, acc_ref = refs[2 * n_pg:]
    j = pl.program_id(1)
    q = q_ref[0]
    bias = bias_ref[...]
    tri = tri_ref[...]
    n_rows = q.shape[0]

    def block(k, v, masked):
        z = lax.dot_general(q, k, (((1,), (1,)), ((), ())), preferred_element_type=F32) + bias
        ls, l1m = _log_sigmoid_pair(z)
        if masked:
            col = lax.broadcasted_iota(jnp.int32, z.shape, 1)
            t_of_row = lax.broadcasted_iota(jnp.int32, z.shape, 0) % t_new
            mask = col < t_of_row
            l1m = jnp.where(mask, l1m, 0.0)
        hi = l1m.astype(BF16)
        lo = (l1m - hi.astype(F32)).astype(BF16)
        after = (jnp.dot(hi, tri, preferred_element_type=F32)
                 + jnp.dot(lo, tri, preferred_element_type=F32) + carry_ref[...])
        a = jnp.exp(ls + after)
        if masked:
            a = jnp.where(mask, a, 0.0)
        acc_ref[...] += jnp.dot(a, v, preferred_element_type=F32)
        carry_ref[...] += jnp.sum(l1m, axis=1, keepdims=True)

    @pl.when(j == 0)
    def _():
        carry_ref[...] = jnp.zeros_like(carry_ref)
        acc_ref[...] = jnp.zeros_like(acc_ref)
        block(kn_ref[0], vn_ref[0], True)

    for i in range(n_pg):
        block(k_refs[i][0], v_refs[i][0], False)

    @pl.when(j == pl.num_programs(1) - 1)
    def _():
        for h in range(N_HEADS):
            blk = acc_ref[h * t_new:(h + 1) * t_new, h * HEAD_DIM:(h + 1) * HEAD_DIM]
            o_ref[0, :, h * HEAD_DIM:(h + 1) * HEAD_DIM] = _rms_scale(blk, axis=1)


def _sample_attention(q, k_new, v_new, bias, cache_k, cache_v, page_table, page_base, n_pg):
    ns, t_new, _ = q.shape
    n_pages = page_table.shape[1]
    n_rows = N_HEADS * t_new
    qh = q.astype(F32).reshape(ns, t_new, N_HEADS, HEAD_DIM).transpose(0, 2, 1, 3)
    eye = jnp.eye(N_HEADS, dtype=F32)
    qbd = (qh[:, :, :, None, :] * eye[None, :, None, :, None]).reshape(ns, n_rows, D_SB)
    bias_col = jnp.repeat(bias, t_new).reshape(n_rows, 1)
    pad = ((0, 0), (0, PAGE - t_new), (0, 0))
    kn = jnp.pad(k_new, pad)
    vn = jnp.pad(v_new, pad)
    jj = lax.broadcasted_iota(jnp.int32, (PAGE, PAGE), 0)
    ss = lax.broadcasted_iota(jnp.int32, (PAGE, PAGE), 1)
    tri = (jj > ss).astype(BF16)
    pt = page_table + page_base

    def page_spec(i):
        return pl.BlockSpec((1, PAGE, D_SB),
                            lambda b, j, pt_: (pt_[b, n_pages - 1 - (j * n_pg + i)], 0, 0))

    per_seq = lambda shape: pl.BlockSpec((1,) + shape, lambda b, j, pt_: (b, 0, 0))
    grid_spec = pltpu.PrefetchScalarGridSpec(
        num_scalar_prefetch=1,
        grid=(ns, n_pages // n_pg),
        in_specs=[pl.BlockSpec((n_rows, 1), lambda b, j, pt_: (0, 0)),
                  per_seq((n_rows, D_SB)), per_seq((PAGE, D_SB)), per_seq((PAGE, D_SB)),
                  pl.BlockSpec((PAGE, PAGE), lambda b, j, pt_: (0, 0))]
                 + [page_spec(i) for i in range(n_pg)] * 2,
        out_specs=per_seq((t_new, D_SB)),
        scratch_shapes=[pltpu.VMEM((n_rows, PAGE), F32), pltpu.VMEM((n_rows, D_SB), F32)],
    )
    return pl.pallas_call(
        functools.partial(_sattn_kernel, n_pg, t_new),
        grid_spec=grid_spec,
        out_shape=jax.ShapeDtypeStruct((ns, t_new, D_SB), F32),
        compiler_params=pltpu.CompilerParams(dimension_semantics=("arbitrary", "arbitrary"),
                                             vmem_limit_bytes=VMEM_LIMIT),
        name="sample_attn",
    )(pt, bias_col, qbd, kn, vn, tri, *([cache_k] * n_pg), *([cache_v] * n_pg))


def _mix_kernel(stride, halo_rows, zero_first_halo, alpha,
                x_ref, u_ref, halo_ref, attn_ref, g1_ref, wdw_ref, bdw_ref, clg_ref, clb_ref,
                gsb_ref, gcv_ref, wout_ref, lg_ref, lb_ref, o_ref, xp_ref):
    tm = u_ref.shape[1]
    halo = halo_ref[0]
    if zero_first_halo:
        halo = jnp.where(pl.program_id(1) == 0, 0.0, halo)
    xp_ref[0:halo_rows] = halo
    xp_ref[halo_rows:halo_rows + tm] = u_ref[0]
    cv = jnp.zeros((tm, u_ref.shape[2]), F32) + bdw_ref[...]
    for j in range(CONV_W):
        off = halo_rows - (CONV_W - 1 - j) * stride
        cv = cv + wdw_ref[j:j + 1, :] * xp_ref[pl.ds(off, tm), :]
    cv = _layernorm(cv, clg_ref[...], clb_ref[...])
    cv = cv * jax.nn.sigmoid(cv)
    cv = _rms_scale(cv, axis=-1) * gcv_ref[...]
    attn = attn_ref[0] * gsb_ref[...]
    d_sb = attn.shape[1]
    mixed = (jnp.dot(attn.astype(BF16), wout_ref[0:d_sb, :], preferred_element_type=F32)
             + jnp.dot(cv.astype(BF16), wout_ref[d_sb:, :], preferred_element_type=F32))
    y = alpha * x_ref[0] + (1.0 + g1_ref[0]) * mixed
    o_ref[0] = _layernorm(y, lg_ref[...], lb_ref[...])


def _mix(x, u, halo, halo_spec, attn, g1, params, tm, stride, halo_rows, zero_first_halo, alpha):
    nb, t, d = x.shape
    d_conv = u.shape[2]
    row = lambda w: pl.BlockSpec((1, tm, w), lambda b, i: (b, i, 0))
    return pl.pallas_call(
        functools.partial(_mix_kernel, stride, halo_rows, zero_first_halo, alpha),
        grid=(nb, t // tm),
        in_specs=[row(d), row(d_conv), halo_spec, row(D_SB), _mod_spec(g1)]
                 + [_const_spec(a) for a in params],
        out_specs=row(d),
        out_shape=jax.ShapeDtypeStruct((nb, t, d), F32),
        scratch_shapes=[pltpu.VMEM((halo_rows + tm, d_conv), F32)],
        compiler_params=pltpu.CompilerParams(dimension_semantics=("arbitrary", "arbitrary"),
                                             vmem_limit_bytes=VMEM_LIMIT),
        name="mix",
    )(x, u, halo, attn, g1, *params)


FFN_CHUNK = 256


def _ffn_kernel(stride, halo_rows, alpha,
                x_ref, sh_ref, sc_ref, g2_ref, halo_ref, wup_ref, wf_ref, bf_ref, wdn_ref,
                lg_ref, lb_ref, o_ref, st_ref, carry_ref, xp_ref, acc_ref):
    tm = x_ref.shape[1]
    d_ff = wdn_ref.shape[0]

    @pl.when(pl.program_id(1) == 0)
    def _():
        carry_ref[...] = halo_ref[0]

    x = x_ref[0]
    h2 = (x * (1.0 + sc_ref[0]) + sh_ref[0]).astype(BF16)
    for c in range(d_ff // FFN_CHUNK):
        lo = c * FFN_CHUNK
        hi = lo + FFN_CHUNK
        val = jnp.dot(h2, wup_ref[:, lo:hi], preferred_element_type=F32)
        gt = jnp.dot(h2, wup_ref[:, d_ff + lo:d_ff + hi], preferred_element_type=F32)
        xp_ref[0:halo_rows] = carry_ref[:, lo:hi]
        xp_ref[halo_rows:halo_rows + tm] = gt
        carry_ref[:, lo:hi] = xp_ref[tm:tm + halo_rows]
        conv = (wf_ref[0:1, lo:hi] * xp_ref[pl.ds(halo_rows - 2 * stride, tm), :]
                + wf_ref[1:2, lo:hi] * xp_ref[pl.ds(halo_rows - stride, tm), :]
                + wf_ref[2:3, lo:hi] * gt + bf_ref[:, lo:hi])
        act = 0.5 * conv * (1.0 + lax.erf(conv * (2.0 ** -0.5))) * val
        contrib = jnp.dot(act.astype(BF16), wdn_ref[lo:hi, :], preferred_element_type=F32)
        if c == 0:
            acc_ref[...] = contrib
        else:
            acc_ref[...] += contrib
    y = alpha * x + (1.0 + g2_ref[0]) * acc_ref[...]
    o_ref[0] = _layernorm(y, lg_ref[...], lb_ref[...])
    st_ref[0] = carry_ref[...]


def _ffn(x, sh, sc, g2, halo0, params, tm, stride, halo_rows, alpha):
    nb, t, d = x.shape
    d_ff = halo0.shape[2]
    row = pl.BlockSpec((1, tm, d), lambda b, i: (b, i, 0))
    st_spec = pl.BlockSpec((1, halo_rows, d_ff), lambda b, i: (b, 0, 0))
    return pl.pallas_call(
        functools.partial(_ffn_kernel, stride, halo_rows, alpha),
        grid=(nb, t // tm),
        in_specs=[row, _mod_spec(sh), _mod_spec(sc), _mod_spec(g2), st_spec]
                 + [_const_spec(a) for a in params],
        out_specs=[row, st_spec],
        out_shape=[jax.ShapeDtypeStruct((nb, t, d), F32),
                   jax.ShapeDtypeStruct((nb, halo_rows, d_ff), F32)],
        scratch_shapes=[pltpu.VMEM((halo_rows, d_ff), F32),
                        pltpu.VMEM((halo_rows + tm, FFN_CHUNK), F32),
                        pltpu.VMEM((tm, d), F32)],
        compiler_params=pltpu.CompilerParams(dimension_semantics=("arbitrary", "arbitrary"),
                                             vmem_limit_bytes=VMEM_LIMIT),
        name="ffn",
    )(x, sh, sc, g2, halo0, *params)


PROMPT_TM = 512
PROMPT_TQ = 256
PROMPT_HALO = 32
SAMPLE_PAGES_PER_STEP = 16


def _time_major(a):
    s, t, c = a.shape
    return a.transpose(1, 0, 2).reshape(1, t * s, c)


def _seq_major(a, s):
    _, ts, c = a.shape
    return a.reshape(ts // s, s, c).transpose(1, 0, 2)


@jax.jit
def kernel(x_prompt, x_sample, cache_k, cache_v, state_conv, state_ffn, page_table, c_prompt, c_sample, w_ada, b_ada, w_in, sb_bias, w_dw, b_dw, cln_g, cln_b, gn_sb, gn_conv, w_out, ln1_g, ln1_b, w_up, w_fdw, b_fdw, w_down, ln2_g, ln2_b):
    depth = w_in.shape[0]
    alpha = (2.0 * depth) ** 0.25
    nb, t, d = x_prompt.shape
    ns, tn, _ = x_sample.shape
    n_pool = cache_k.shape[1]
    d_ff = w_down.shape[1]
    d_conv = w_dw.shape[2]
    ck = cache_k.reshape(depth * n_pool, PAGE, D_SB)
    cvv = cache_v.reshape(depth * n_pool, PAGE, D_SB)
    vec = lambda a: a.reshape(1, -1)

    xp = x_prompt
    xs = _time_major(x_sample)
    outs = [[] for _ in range(8)]
    for layer in range(depth):
        w_in_bf = w_in[layer].astype(BF16)
        w_out_bf = w_out[layer].astype(BF16)
        w_up_bf = w_up[layer].astype(BF16)
        w_down_bf = w_down[layer].astype(BF16)
        mix_params = (w_dw[layer], vec(b_dw[layer]), vec(cln_g[layer]), vec(cln_b[layer]),
                      vec(gn_sb[layer]), vec(gn_conv[layer]), w_out_bf,
                      vec(ln1_g[layer]), vec(ln1_b[layer]))
        ffn_params = (w_up_bf, w_fdw[layer], vec(b_fdw[layer]), w_down_bf,
                      vec(ln2_g[layer]), vec(ln2_b[layer]))

        mod = _ada(jnp.concatenate([c_prompt, c_sample], axis=0), w_ada[layer], b_ada[layer])
        mod_p = [m.reshape(nb, 1, d) for m in jnp.split(mod[:nb], 6, axis=-1)]
        mod_s = [jnp.tile(m, (tn, 1)).reshape(1, tn * ns, d)
                 for m in jnp.split(mod[nb:], 6, axis=-1)]

        q, k, v, u = _inproj(xp, mod_p[0], mod_p[1], w_in_bf, PROMPT_TM)
        attn = _prompt_attention(q, k, v, sb_bias[layer], PROMPT_TQ)
        halo_blocks = PROMPT_TM // PROMPT_HALO
        halo_spec = pl.BlockSpec((1, PROMPT_HALO, d_conv),
                                 lambda b, i: (b, jnp.maximum(i * halo_blocks - 1, 0), 0))
        x1 = _mix(xp, u, u, halo_spec, attn, mod_p[2], mix_params, PROMPT_TM, 1, PROMPT_HALO,
                  True, alpha)
        xp, fst = _ffn(x1, mod_p[3], mod_p[4], mod_p[5], jnp.zeros((nb, SUBLANES, d_ff), F32),
                       ffn_params, PROMPT_TM, 1, SUBLANES, alpha)
        outs[0].append(k.reshape(nb, t, N_HEADS, HEAD_DIM))
        outs[1].append(v.reshape(nb, t, N_HEADS, HEAD_DIM))
        outs[2].append(u[:, t - (CONV_W - 1):])
        outs[3].append(fst[:, SUBLANES - (FFN_CONV_W - 1):])

        rows = tn * ns
        q, k, v, u = _inproj(xs, mod_s[0], mod_s[1], w_in_bf, rows)
        k_sm = _seq_major(k, ns)
        v_sm = _seq_major(v, ns)
        attn = _sample_attention(_seq_major(q, ns), k_sm, v_sm, sb_bias[layer], ck, cvv,
                                 page_table, layer * n_pool, SAMPLE_PAGES_PER_STEP)
        conv_prev = _time_major(state_conv[layer])
        conv_rows = (CONV_W - 1) * ns
        halo_spec = pl.BlockSpec((1, conv_rows, d_conv), lambda b, i: (0, 0, 0))
        x1 = _mix(xs, u, conv_prev, halo_spec, _time_major(attn), mod_s[2], mix_params, rows, ns,
                  conv_rows, False, alpha)
        ffn_rows = (FFN_CONV_W - 1) * ns
        xs, fst = _ffn(x1, mod_s[3], mod_s[4], mod_s[5], _time_major(state_ffn[layer]),
                       ffn_params, rows, ns, ffn_rows, alpha)
        outs[4].append(k_sm.reshape(ns, tn, N_HEADS, HEAD_DIM))
        outs[5].append(v_sm.reshape(ns, tn, N_HEADS, HEAD_DIM))
        conv_all = jnp.concatenate([conv_prev, u], axis=1)
        outs[6].append(_seq_major(conv_all[:, conv_all.shape[1] - conv_rows:], ns))
        outs[7].append(_seq_major(fst, ns))

    y_sample = _seq_major(xs, ns)
    k_p, v_p, c_p, f_p, k_s, v_s, c_s, f_s = [jnp.stack(o) for o in outs]
    return (xp, y_sample, k_p, v_p, c_p, f_p, k_s, v_s, c_s, f_s)
```

```python
import functools

import jax
import jax.numpy as jnp
from jax import lax
from jax.experimental import pallas as pl
from jax.experimental.pallas import tpu as pltpu

N_HEADS = 8
HEAD_DIM = 64
D_SB = N_HEADS * HEAD_DIM
CONV_W = 31
FFN_CONV_W = 3
PAGE = 128
LN_EPS = 1e-5

LANES = 128
SUBLANES = 8
KEY_BLOCK = LANES
RUN = KEY_BLOCK // SUBLANES
VMEM_LIMIT = 56 * 1024 * 1024

F32 = jnp.float32
BF16 = jnp.bfloat16


def _layernorm(x, g, b):
    mu = jnp.mean(x, axis=-1, keepdims=True)
    xc = x - mu
    var = jnp.mean(xc * xc, axis=-1, keepdims=True)
    return xc * lax.rsqrt(var + LN_EPS) * g + b


def _rms_scale(x, axis):
    ms = jnp.mean(x * x, axis=axis, keepdims=True)
    return x * lax.rsqrt(ms + LN_EPS)


LOG2E = 1.4426950408889634
Q_SCALE = -(HEAD_DIM ** -0.5) * LOG2E


def _log2_sigmoid(y):
    return jnp.minimum(y, 0.0) - jnp.log2(1.0 + jnp.exp2(-jnp.abs(y)))


def _ada_kernel(c_ref, w_ref, b_ref, o_ref):
    c = c_ref[...]
    s = c * jax.nn.sigmoid(c)
    o_ref[...] = jnp.dot(s, w_ref[...], preferred_element_type=F32) + b_ref[...]


def _ada(c, w, b):
    n, d = c.shape
    n_out = w.shape[1]
    tn = 1536
    return pl.pallas_call(
        _ada_kernel,
        grid=(n_out // tn,),
        in_specs=[pl.BlockSpec((n, d), lambda j: (0, 0)),
                  pl.BlockSpec((d, tn), lambda j: (0, j)),
                  pl.BlockSpec((1, tn), lambda j: (0, j))],
        out_specs=pl.BlockSpec((n, tn), lambda j: (0, j)),
        out_shape=jax.ShapeDtypeStruct((n, n_out), F32),
        compiler_params=pltpu.CompilerParams(dimension_semantics=("arbitrary",),
                                             vmem_limit_bytes=VMEM_LIMIT),
        name="ada",
    )(c, w, b.reshape(1, n_out))


def _inproj_kernel(q_scale, x_ref, sh_ref, sc_ref, w_ref, q_ref, k_ref, v_ref, u_ref):
    h = x_ref[0] * (1.0 + sc_ref[0]) + sh_ref[0]
    proj = jnp.dot(h.astype(BF16), w_ref[...], preferred_element_type=F32)
    q_ref[0] = (proj[:, :D_SB] * q_scale).astype(BF16)
    k_ref[0] = proj[:, D_SB:2 * D_SB]
    v_ref[0] = proj[:, 2 * D_SB:3 * D_SB]
    d_conv = (proj.shape[1] - 3 * D_SB) // 2
    a = proj[:, 3 * D_SB:3 * D_SB + d_conv]
    g = proj[:, 3 * D_SB + d_conv:]
    u_ref[0] = a * jax.nn.sigmoid(g)


def _mod_spec(m):
    return pl.BlockSpec((1,) + m.shape[1:], lambda b, i: (b, 0, 0))


def _const_spec(a):
    nd = a.ndim
    return pl.BlockSpec(a.shape, lambda b, i: (0,) * nd)


def _inproj(x, sh, sc, w_in_bf, tm):
    nb, t, d = x.shape
    d_conv = (w_in_bf.shape[1] - 3 * D_SB) // 2
    row = lambda w: pl.BlockSpec((1, tm, w), lambda b, i: (b, i, 0))
    return pl.pallas_call(
        functools.partial(_inproj_kernel, Q_SCALE),
        grid=(nb, t // tm),
        in_specs=[row(d), _mod_spec(sh), _mod_spec(sc), _const_spec(w_in_bf)],
        out_specs=[row(D_SB), row(D_SB), row(D_SB), row(d_conv)],
        out_shape=[jax.ShapeDtypeStruct((nb, t, D_SB), BF16),
                   jax.ShapeDtypeStruct((nb, t, D_SB), F32),
                   jax.ShapeDtypeStruct((nb, t, D_SB), F32),
                   jax.ShapeDtypeStruct((nb, t, d_conv), F32)],
        compiler_params=pltpu.CompilerParams(dimension_semantics=("arbitrary", "arbitrary"),
                                             vmem_limit_bytes=VMEM_LIMIT),
        name="inproj",
    )(x, sh, sc, w_in_bf)


STEP_KEYS = 2 * KEY_BLOCK


def _suffix_sum_sublanes(x):
    s = lax.broadcasted_iota(jnp.int32, x.shape, 0)
    for sh in (1, 2, 4):
        x = x + jnp.where(s + sh < SUBLANES, pltpu.roll(x, SUBLANES - sh, axis=0), 0.0)
    return x


def _pattn_kernel(tq, bias_ref, qt_ref, kp_ref, vt_ref, o_ref, y_buf, a_buf, acc_ref):
    p = pl.program_id(1)
    i = pl.program_id(2)
    n_sub = STEP_KEYS // KEY_BLOCK
    n_vreg_rows = KEY_BLOCK // SUBLANES
    n_col = 2 * tq // LANES
    qt = qt_ref[0, 0]
    row = lax.broadcasted_iota(jnp.int32, qt.shape, 0)
    zero = jnp.zeros_like(qt)
    qcat = jnp.concatenate([jnp.where(row < HEAD_DIM, qt, zero),
                            jnp.where(row >= HEAD_DIM, qt, zero)], axis=1)
    bias = [jnp.full((SUBLANES, LANES), bias_ref[2 * p + h] * LOG2E, F32) for h in range(2)]
    diag = (lax.broadcasted_iota(jnp.int32, (SUBLANES, LANES), 1)
            - RUN * lax.broadcasted_iota(jnp.int32, (SUBLANES, LANES), 0))

    def logits(kb):
        return jnp.dot(kp_ref[0, 0, kb], qcat, preferred_element_type=F32)

    def weights(carry, masked):
        new_carry = []
        for c in range(n_col):
            lanes = slice(c * LANES, (c + 1) * LANES)
            q_off = (c * LANES) % tq
            tot = carry[c]
            for sub in range(n_sub - 1, -1, -1):
                t_part = [None] * n_vreg_rows
                mask = [None] * n_vreg_rows
                run = jnp.zeros((SUBLANES, LANES), F32)
                for ii in range(n_vreg_rows - 1, -1, -1):
                    r0 = sub * KEY_BLOCK + ii * SUBLANES
                    y = y_buf[r0:r0 + SUBLANES, lanes] - bias[c * LANES // tq]
                    l1m = _log2_sigmoid(y)
                    if masked:
                        mask[ii] = (sub * KEY_BLOCK + ii - q_off) < diag
                        l1m = jnp.where(mask[ii], l1m, 0.0)
                    run = run + l1m
                    t_part[ii] = run - y
                incl = _suffix_sum_sublanes(run)
                off = incl - run + tot
                tot = tot + jnp.broadcast_to(incl[0:1], incl.shape)
                for ii in range(0, n_vreg_rows, 2):
                    pair = []
                    for jj in (ii, ii + 1):
                        a = jnp.exp2(t_part[jj] + off)
                        if masked:
                            a = jnp.where(mask[jj], a, 0.0)
                        pair.append(a)
                    r0 = sub * KEY_BLOCK + ii * SUBLANES
                    a_buf[r0:r0 + 2 * SUBLANES, lanes] = jnp.concatenate(pair, axis=0).astype(BF16)
            new_carry.append(tot)
        return tuple(new_carry)

    def accumulate(kb):
        vt = vt_ref[0, 0, kb]
        acc_ref[0] += jnp.dot(vt[:HEAD_DIM], a_buf[:, :tq], preferred_element_type=F32)
        acc_ref[1] += jnp.dot(vt[HEAD_DIM:], a_buf[:, tq:], preferred_element_type=F32)

    y_buf[...] = logits(i)
    acc_ref[...] = jnp.zeros_like(acc_ref)
    y_next = logits(jnp.maximum(i - 1, 0))
    carry = weights((jnp.zeros((SUBLANES, LANES), F32),) * n_col, True)
    y_buf[...] = y_next

    def body(t, carry):
        y_next = logits(jnp.maximum(i - t - 1, 0))
        accumulate(i - t + 1)
        carry = weights(carry, False)
        y_buf[...] = y_next
        return carry

    lax.fori_loop(1, i + 1, body, carry)
    accumulate(0)
    o_ref[0] = jnp.concatenate([_rms_scale(acc_ref[0], axis=0),
                                _rms_scale(acc_ref[1], axis=0)], axis=0).T


def _prompt_attention(q, k, v, bias, tq):
    assert tq == STEP_KEYS
    nb, t, _ = q.shape
    nkb = t // STEP_KEYS
    n_sub = STEP_KEYS // KEY_BLOCK
    n_pairs = D_SB // LANES
    qt = q.reshape(nb, t, n_pairs, LANES).transpose(0, 2, 3, 1)
    kb_ = k.astype(BF16).reshape(nb, nkb, n_sub, SUBLANES, RUN, n_pairs, LANES)
    kp = kb_.transpose(0, 5, 1, 2, 4, 3, 6).reshape(nb, n_pairs, nkb, STEP_KEYS, LANES)
    vb_ = v.astype(BF16).reshape(nb, nkb, n_sub, SUBLANES, RUN, n_pairs, LANES)
    vt = vb_.transpose(0, 5, 1, 6, 2, 4, 3).reshape(nb, n_pairs, nkb, LANES, STEP_KEYS)
    k_spec = pl.BlockSpec((1, 1, nkb, STEP_KEYS, LANES), lambda b, p, i: (b, p, 0, 0, 0))
    v_spec = pl.BlockSpec((1, 1, nkb, LANES, STEP_KEYS), lambda b, p, i: (b, p, 0, 0, 0))
    return pl.pallas_call(
        functools.partial(_pattn_kernel, tq),
        grid=(nb, n_pairs, t // tq),
        in_specs=[pl.BlockSpec(memory_space=pltpu.SMEM),
                  pl.BlockSpec((1, 1, LANES, tq), lambda b, p, i: (b, p, 0, i)),
                  k_spec, v_spec],
        out_specs=pl.BlockSpec((1, tq, LANES), lambda b, p, i: (b, i, p)),
        out_shape=jax.ShapeDtypeStruct((nb, t, D_SB), F32),
        scratch_shapes=[pltpu.VMEM((STEP_KEYS, 2 * tq), F32),
                        pltpu.VMEM((STEP_KEYS, 2 * tq), BF16),
                        pltpu.VMEM((2, HEAD_DIM, tq), F32)],
        compiler_params=pltpu.CompilerParams(
            dimension_semantics=("arbitrary", "arbitrary", "arbitrary"),
            vmem_limit_bytes=VMEM_LIMIT),
        name="prompt_attn",
    )(bias, qt, kp, vt)


def _sattn_kernel(n_pg, t_new, pt_ref, bias_ref, q_ref, kn_ref, vn_ref, tri_ref, *refs):
    k_refs = refs[:n_pg]
    v_refs = refs[n_pg:2 * n_pg]
    o_ref, carry_ref, acc_ref = refs[2 * n_pg:]
    j = pl.program_id(1)
    q = q_ref[0]
    bias = bias_ref[...] * LOG2E
    tri = tri_ref[...]
    nt = (((1,), (1,)), ((), ()))

    def blocks(kts, vts, masked):
        ys = [jnp.dot(q, kt, preferred_element_type=F32) - bias for kt in kts]
        l1ms = [_log2_sigmoid(y) for y in ys]
        lss = [l1m - y for l1m, y in zip(l1ms, ys)]
        if masked:
            col = lax.broadcasted_iota(jnp.int32, ys[0].shape, 1)
            t_of_row = lax.broadcasted_iota(jnp.int32, ys[0].shape, 0) % t_new
            mask = col < t_of_row
            l1ms = [jnp.where(mask, l1m, 0.0) for l1m in l1ms]
        his = [l1m.astype(BF16) for l1m in l1ms]
        los = [(l1m - hi.astype(F32)).astype(BF16) for l1m, hi in zip(l1ms, his)]
        afters = [jnp.dot(hi, tri, preferred_element_type=F32)
                  + jnp.dot(lo, tri, preferred_element_type=F32) for hi, lo in zip(his, los)]
        sums = [jnp.sum(l1m, axis=1, keepdims=True) for l1m in l1ms]
        carry = carry_ref[...]
        pv = None
        for ls, after, tot, vt in zip(lss, afters, sums, vts):
            a = jnp.exp2(ls + after + carry)
            if masked:
                a = jnp.where(mask, a, 0.0)
            carry = carry + tot
            contrib = lax.dot_general(a, vt, nt, preferred_element_type=F32)
            pv = contrib if pv is None else pv + contrib
        carry_ref[...] = carry
        return pv

    @pl.when(j == 0)
    def _():
        carry_ref[...] = jnp.zeros_like(carry_ref)
        acc_ref[...] = blocks([kn_ref[0].reshape(D_SB, PAGE)], [vn_ref[0].reshape(D_SB, PAGE)], True)

    acc_ref[...] += blocks([r[0, 0].reshape(D_SB, PAGE) for r in k_refs],
                           [r[0, 0].reshape(D_SB, PAGE) for r in v_refs], False)

    @pl.when(j == pl.num_programs(1) - 1)
    def _():
        for h in range(N_HEADS):
            blk = acc_ref[h * t_new:(h + 1) * t_new, h * HEAD_DIM:(h + 1) * HEAD_DIM]
            o_ref[0, :, h * HEAD_DIM:(h + 1) * HEAD_DIM] = _rms_scale(blk, axis=1)


def _sample_attention(q, k_new, v_new, bias, cache_kt, cache_vt, page_table, layer, n_pg):
    ns, t_new, _ = q.shape
    n_pages = page_table.shape[1]
    n_rows = N_HEADS * t_new
    qh = q.astype(F32).reshape(ns, t_new, N_HEADS, HEAD_DIM).transpose(0, 2, 1, 3)
    eye = jnp.eye(N_HEADS, dtype=F32)
    qbd = (qh[:, :, :, None, :] * eye[None, :, None, :, None]).reshape(ns, n_rows, D_SB)
    bias_col = jnp.repeat(bias, t_new).reshape(n_rows, 1)
    pad = ((0, 0), (0, 0), (0, 0), (0, PAGE - t_new))
    kn = jnp.pad(k_new.reshape(ns, t_new, N_HEADS, HEAD_DIM).transpose(0, 2, 3, 1), pad)
    vn = jnp.pad(v_new.reshape(ns, t_new, N_HEADS, HEAD_DIM).transpose(0, 2, 3, 1), pad)
    jj = lax.broadcasted_iota(jnp.int32, (PAGE, PAGE), 0)
    ss = lax.broadcasted_iota(jnp.int32, (PAGE, PAGE), 1)
    tri = (jj > ss).astype(BF16)

    def page_spec(i):
        return pl.BlockSpec(
            (1, 1, N_HEADS, HEAD_DIM, PAGE),
            lambda b, j, pt_: (layer, pt_[b, n_pages - 1 - (j * n_pg + i)], 0, 0, 0))

    per_seq = lambda shape: pl.BlockSpec((1,) + shape, lambda b, j, pt_: (b,) + (0,) * len(shape))
    grid_spec = pltpu.PrefetchScalarGridSpec(
        num_scalar_prefetch=1,
        grid=(ns, n_pages // n_pg),
        in_specs=[pl.BlockSpec((n_rows, 1), lambda b, j, pt_: (0, 0)),
                  per_seq((n_rows, D_SB)),
                  per_seq((N_HEADS, HEAD_DIM, PAGE)), per_seq((N_HEADS, HEAD_DIM, PAGE)),
                  pl.BlockSpec((PAGE, PAGE), lambda b, j, pt_: (0, 0))]
                 + [page_spec(i) for i in range(n_pg)] * 2,
        out_specs=per_seq((t_new, D_SB)),
        scratch_shapes=[pltpu.VMEM((n_rows, PAGE), F32), pltpu.VMEM((n_rows, D_SB), F32)],
    )
    return pl.pallas_call(
        functools.partial(_sattn_kernel, n_pg, t_new),
        grid_spec=grid_spec,
        out_shape=jax.ShapeDtypeStruct((ns, t_new, D_SB), F32),
        compiler_params=pltpu.CompilerParams(dimension_semantics=("arbitrary", "arbitrary"),
                                             vmem_limit_bytes=VMEM_LIMIT),
        name="sample_attn",
    )(page_table, bias_col, qbd, kn, vn, tri, *([cache_kt] * n_pg), *([cache_vt] * n_pg))


def _mix_kernel(stride, halo_rows, zero_first_halo, alpha,
                x_ref, u_ref, halo_ref, attn_ref, g1_ref, wdw_ref, bdw_ref, clg_ref, clb_ref,
                gsb_ref, gcv_ref, wout_ref, lg_ref, lb_ref, o_ref, xp_ref):
    tm = u_ref.shape[1]
    halo = halo_ref[0]
    if zero_first_halo:
        halo = jnp.where(pl.program_id(1) == 0, 0.0, halo)
    xp_ref[0:halo_rows] = halo
    xp_ref[halo_rows:halo_rows + tm] = u_ref[0]
    cv = jnp.zeros((tm, u_ref.shape[2]), F32) + bdw_ref[...]
    for j in range(CONV_W):
        off = halo_rows - (CONV_W - 1 - j) * stride
        cv = cv + wdw_ref[j:j + 1, :] * xp_ref[pl.ds(off, tm), :]
    cv = _layernorm(cv, clg_ref[...], clb_ref[...])
    cv = cv * jax.nn.sigmoid(cv)
    cv = _rms_scale(cv, axis=-1) * gcv_ref[...]
    attn = attn_ref[0] * gsb_ref[...]
    d_sb = attn.shape[1]
    mixed = (jnp.dot(attn.astype(BF16), wout_ref[0:d_sb, :], preferred_element_type=F32)
             + jnp.dot(cv.astype(BF16), wout_ref[d_sb:, :], preferred_element_type=F32))
    y = alpha * x_ref[0] + (1.0 + g1_ref[0]) * mixed
    o_ref[0] = _layernorm(y, lg_ref[...], lb_ref[...])


def _mix(x, u, halo, halo_spec, attn, g1, params, tm, stride, halo_rows, zero_first_halo, alpha):
    nb, t, d = x.shape
    d_conv = u.shape[2]
    row = lambda w: pl.BlockSpec((1, tm, w), lambda b, i: (b, i, 0))
    return pl.pallas_call(
        functools.partial(_mix_kernel, stride, halo_rows, zero_first_halo, alpha),
        grid=(nb, t // tm),
        in_specs=[row(d), row(d_conv), halo_spec, row(D_SB), _mod_spec(g1)]
                 + [_const_spec(a) for a in params],
        out_specs=row(d),
        out_shape=jax.ShapeDtypeStruct((nb, t, d), F32),
        scratch_shapes=[pltpu.VMEM((halo_rows + tm, d_conv), F32)],
        compiler_params=pltpu.CompilerParams(dimension_semantics=("arbitrary", "arbitrary"),
                                             vmem_limit_bytes=VMEM_LIMIT),
        name="mix",
    )(x, u, halo, attn, g1, *params)


FFN_CHUNK = 256


def _ffn_kernel(stride, halo_rows, alpha,
                x_ref, sh_ref, sc_ref, g2_ref, halo_ref, wup_ref, wf_ref, bf_ref, wdn_ref,
                lg_ref, lb_ref, o_ref, st_ref, carry_ref, xp_ref, acc_ref):
    tm = x_ref.shape[1]
    d_ff = wdn_ref.shape[0]

    @pl.when(pl.program_id(1) == 0)
    def _():
        carry_ref[...] = halo_ref[0]

    x = x_ref[0]
    h2 = (x * (1.0 + sc_ref[0]) + sh_ref[0]).astype(BF16)
    for c in range(d_ff // FFN_CHUNK):
        lo = c * FFN_CHUNK
        hi = lo + FFN_CHUNK
        val = jnp.dot(h2, wup_ref[:, lo:hi], preferred_element_type=F32)
        gt = jnp.dot(h2, wup_ref[:, d_ff + lo:d_ff + hi], preferred_element_type=F32)
        xp_ref[0:halo_rows] = carry_ref[:, lo:hi]
        xp_ref[halo_rows:halo_rows + tm] = gt
        carry_ref[:, lo:hi] = xp_ref[tm:tm + halo_rows]
        conv = (wf_ref[0:1, lo:hi] * xp_ref[pl.ds(halo_rows - 2 * stride, tm), :]
                + wf_ref[1:2, lo:hi] * xp_ref[pl.ds(halo_rows - stride, tm), :]
                + wf_ref[2:3, lo:hi] * gt + bf_ref[:, lo:hi])
        act = 0.5 * conv * (1.0 + lax.erf(conv * (2.0 ** -0.5))) * val
        contrib = jnp.dot(act.astype(BF16), wdn_ref[lo:hi, :], preferred_element_type=F32)
        if c == 0:
            acc_ref[...] = contrib
        else:
            acc_ref[...] += contrib
    y = alpha * x + (1.0 + g2_ref[0]) * acc_ref[...]
    o_ref[0] = _layernorm(y, lg_ref[...], lb_ref[...])
    st_ref[0] = carry_ref[...]


def _ffn(x, sh, sc, g2, halo0, params, tm, stride, halo_rows, alpha):
    nb, t, d = x.shape
    d_ff = halo0.shape[2]
    row = pl.BlockSpec((1, tm, d), lambda b, i: (b, i, 0))
    st_spec = pl.BlockSpec((1, halo_rows, d_ff), lambda b, i: (b, 0, 0))
    return pl.pallas_call(
        functools.partial(_ffn_kernel, stride, halo_rows, alpha),
        grid=(nb, t // tm),
        in_specs=[row, _mod_spec(sh), _mod_spec(sc), _mod_spec(g2), st_spec]
                 + [_const_spec(a) for a in params],
        out_specs=[row, st_spec],
        out_shape=[jax.ShapeDtypeStruct((nb, t, d), F32),
                   jax.ShapeDtypeStruct((nb, halo_rows, d_ff), F32)],
        scratch_shapes=[pltpu.VMEM((halo_rows, d_ff), F32),
                        pltpu.VMEM((halo_rows + tm, FFN_CHUNK), F32),
                        pltpu.VMEM((tm, d), F32)],
        compiler_params=pltpu.CompilerParams(dimension_semantics=("arbitrary", "arbitrary"),
                                             vmem_limit_bytes=VMEM_LIMIT),
        name="ffn",
    )(x, sh, sc, g2, halo0, *params)


PROMPT_TM = 512
PROMPT_TQ = 256
PROMPT_HALO = 32
SAMPLE_PAGES_PER_STEP = 16


def _time_major(a):
    s, t, c = a.shape
    return a.transpose(1, 0, 2).reshape(1, t * s, c)


def _seq_major(a, s):
    _, ts, c = a.shape
    return a.reshape(ts // s, s, c).transpose(1, 0, 2)


@jax.jit
def kernel(x_prompt, x_sample, cache_k, cache_v, state_conv, state_ffn, page_table, c_prompt, c_sample, w_ada, b_ada, w_in, sb_bias, w_dw, b_dw, cln_g, cln_b, gn_sb, gn_conv, w_out, ln1_g, ln1_b, w_up, w_fdw, b_fdw, w_down, ln2_g, ln2_b):
    depth = w_in.shape[0]
    alpha = (2.0 * depth) ** 0.25
    nb, t, d = x_prompt.shape
    ns, tn, _ = x_sample.shape
    d_ff = w_down.shape[1]
    d_conv = w_dw.shape[2]
    cache_kt = jnp.transpose(cache_k, (0, 1, 3, 4, 2))
    cache_vt = jnp.transpose(cache_v, (0, 1, 3, 4, 2))
    vec = lambda a: a.reshape(1, -1)

    xp = x_prompt
    xs = _time_major(x_sample)
    outs = [[] for _ in range(8)]
    for layer in range(depth):
        w_in_bf = w_in[layer].astype(BF16)
        w_out_bf = w_out[layer].astype(BF16)
        w_up_bf = w_up[layer].astype(BF16)
        w_down_bf = w_down[layer].astype(BF16)
        mix_params = (w_dw[layer], vec(b_dw[layer]), vec(cln_g[layer]), vec(cln_b[layer]),
                      vec(gn_sb[layer]), vec(gn_conv[layer]), w_out_bf,
                      vec(ln1_g[layer]), vec(ln1_b[layer]))
        ffn_params = (w_up_bf, w_fdw[layer], vec(b_fdw[layer]), w_down_bf,
                      vec(ln2_g[layer]), vec(ln2_b[layer]))

        mod = _ada(jnp.concatenate([c_prompt, c_sample], axis=0), w_ada[layer], b_ada[layer])
        mod_p = [m.reshape(nb, 1, d) for m in jnp.split(mod[:nb], 6, axis=-1)]
        mod_s = [jnp.tile(m, (tn, 1)).reshape(1, tn * ns, d)
                 for m in jnp.split(mod[nb:], 6, axis=-1)]

        q, k, v, u = _inproj(xp, mod_p[0], mod_p[1], w_in_bf, PROMPT_TM)
        attn = _prompt_attention(q, k, v, sb_bias[layer], PROMPT_TQ)
        halo_blocks = PROMPT_TM // PROMPT_HALO
        halo_spec = pl.BlockSpec((1, PROMPT_HALO, d_conv),
                                 lambda b, i: (b, jnp.maximum(i * halo_blocks - 1, 0), 0))
        x1 = _mix(xp, u, u, halo_spec, attn, mod_p[2], mix_params, PROMPT_TM, 1, PROMPT_HALO,
                  True, alpha)
        xp, fst = _ffn(x1, mod_p[3], mod_p[4], mod_p[5], jnp.zeros((nb, SUBLANES, d_ff), F32),
                       ffn_params, PROMPT_TM, 1, SUBLANES, alpha)
        outs[0].append(k.reshape(nb, t, N_HEADS, HEAD_DIM))
        outs[1].append(v.reshape(nb, t, N_HEADS, HEAD_DIM))
        outs[2].append(u[:, t - (CONV_W - 1):])
        outs[3].append(fst[:, SUBLANES - (FFN_CONV_W - 1):])

        rows = tn * ns
        q, k, v, u = _inproj(xs, mod_s[0], mod_s[1], w_in_bf, rows)
        k_sm = _seq_major(k, ns)
        v_sm = _seq_major(v, ns)
        attn = _sample_attention(_seq_major(q, ns), k_sm, v_sm, sb_bias[layer], cache_kt,
                                 cache_vt, page_table, layer, SAMPLE_PAGES_PER_STEP)
        conv_prev = _time_major(state_conv[layer])
        conv_rows = (CONV_W - 1) * ns
        halo_spec = pl.BlockSpec((1, conv_rows, d_conv), lambda b, i: (0, 0, 0))
        x1 = _mix(xs, u, conv_prev, halo_spec, _time_major(attn), mod_s[2], mix_params, rows, ns,
                  conv_rows, False, alpha)
        ffn_rows = (FFN_CONV_W - 1) * ns
        xs, fst = _ffn(x1, mod_s[3], mod_s[4], mod_s[5], _time_major(state_ffn[layer]),
                       ffn_params, rows, ns, ffn_rows, alpha)
        outs[4].append(k_sm.reshape(ns, tn, N_HEADS, HEAD_DIM))
        outs[5].append(v_sm.reshape(ns, tn, N_HEADS, HEAD_DIM))
        conv_all = jnp.concatenate([conv_prev, u], axis=1)
        outs[6].append(_seq_major(conv_all[:, conv_all.shape[1] - conv_rows:], ns))
        outs[7].append(_seq_major(fst, ns))

    y_sample = _seq_major(xs, ns)
    k_p, v_p, c_p, f_p, k_s, v_s, c_s, f_s = [jnp.stack(o) for o in outs]
    return (xp, y_sample, k_p, v_p, c_p, f_p, k_s, v_s, c_s, f_s)
```

```python
import functools

import jax
import jax.numpy as jnp
from jax import lax
from jax.experimental import pallas as pl
from jax.experimental.pallas import tpu as pltpu

N_HEADS = 8
HEAD_DIM = 64
D_SB = N_HEADS * HEAD_DIM
CONV_W = 31
FFN_CONV_W = 3
PAGE = 128
LN_EPS = 1e-5

LANES = 128
SUBLANES = 8
KEY_BLOCK = LANES
RUN = KEY_BLOCK // SUBLANES
VMEM_LIMIT = 56 * 1024 * 1024

F32 = jnp.float32
BF16 = jnp.bfloat16


def _layernorm(x, g, b):
    mu = jnp.mean(x, axis=-1, keepdims=True)
    xc = x - mu
    var = jnp.mean(xc * xc, axis=-1, keepdims=True)
    return xc * lax.rsqrt(var + LN_EPS) * g + b


def _rms_scale(x, axis):
    ms = jnp.mean(x * x, axis=axis, keepdims=True)
    return x * lax.rsqrt(ms + LN_EPS)


LOG2E = 1.4426950408889634
Q_SCALE = -(HEAD_DIM ** -0.5) * LOG2E
UNDERFLOW_LOG2 = -160.0


def _log2_sigmoid(y):
    return jnp.minimum(y, 0.0) - jnp.log2(1.0 + jnp.exp2(-jnp.abs(y)))


def _ada_kernel(c_ref, w_ref, b_ref, o_ref):
    c = c_ref[...]
    s = c * jax.nn.sigmoid(c)
    o_ref[...] = jnp.dot(s, w_ref[...], preferred_element_type=F32) + b_ref[...]


def _ada(c, w, b):
    n, d = c.shape
    n_out = w.shape[1]
    tn = 1536
    return pl.pallas_call(
        _ada_kernel,
        grid=(n_out // tn,),
        in_specs=[pl.BlockSpec((n, d), lambda j: (0, 0)),
                  pl.BlockSpec((d, tn), lambda j: (0, j)),
                  pl.BlockSpec((1, tn), lambda j: (0, j))],
        out_specs=pl.BlockSpec((n, tn), lambda j: (0, j)),
        out_shape=jax.ShapeDtypeStruct((n, n_out), F32),
        compiler_params=pltpu.CompilerParams(dimension_semantics=("arbitrary",),
                                             vmem_limit_bytes=VMEM_LIMIT),
        name="ada",
    )(c, w, b.reshape(1, n_out))


def _inproj_kernel(q_scale, x_ref, sh_ref, sc_ref, w_ref, q_ref, k_ref, v_ref, u_ref):
    h = x_ref[0] * (1.0 + sc_ref[0]) + sh_ref[0]
    proj = jnp.dot(h.astype(BF16), w_ref[...], preferred_element_type=F32)
    q_ref[0] = (proj[:, :D_SB] * q_scale).astype(BF16)
    k_ref[0] = proj[:, D_SB:2 * D_SB]
    v_ref[0] = proj[:, 2 * D_SB:3 * D_SB]
    d_conv = (proj.shape[1] - 3 * D_SB) // 2
    a = proj[:, 3 * D_SB:3 * D_SB + d_conv]
    g = proj[:, 3 * D_SB + d_conv:]
    u_ref[0] = a * jax.nn.sigmoid(g)


def _mod_spec(m):
    return pl.BlockSpec((1,) + m.shape[1:], lambda b, i: (b, 0, 0))


def _const_spec(a):
    nd = a.ndim
    return pl.BlockSpec(a.shape, lambda b, i: (0,) * nd, pipeline_mode=pl.Buffered(1))


def _inproj(x, sh, sc, w_in_bf, tm):
    nb, t, d = x.shape
    d_conv = (w_in_bf.shape[1] - 3 * D_SB) // 2
    row = lambda w: pl.BlockSpec((1, tm, w), lambda b, i: (b, i, 0))
    return pl.pallas_call(
        functools.partial(_inproj_kernel, Q_SCALE),
        grid=(nb, t // tm),
        in_specs=[row(d), _mod_spec(sh), _mod_spec(sc), _const_spec(w_in_bf)],
        out_specs=[row(D_SB), row(D_SB), row(D_SB), row(d_conv)],
        out_shape=[jax.ShapeDtypeStruct((nb, t, D_SB), BF16),
                   jax.ShapeDtypeStruct((nb, t, D_SB), F32),
                   jax.ShapeDtypeStruct((nb, t, D_SB), F32),
                   jax.ShapeDtypeStruct((nb, t, d_conv), F32)],
        compiler_params=pltpu.CompilerParams(dimension_semantics=("arbitrary", "arbitrary"),
                                             vmem_limit_bytes=VMEM_LIMIT),
        name="inproj",
    )(x, sh, sc, w_in_bf)


STEP_KEYS = 2 * KEY_BLOCK


def _suffix_sum_sublanes(x):
    s = lax.broadcasted_iota(jnp.int32, x.shape, 0)
    for sh in (1, 2, 4):
        x = x + jnp.where(s + sh < SUBLANES, pltpu.roll(x, SUBLANES - sh, axis=0), 0.0)
    return x


def _pattn_kernel(tq, bias_ref, qt_ref, kp_ref, vt_ref, o_ref, y_buf, a_buf, acc_ref):
    p = pl.program_id(1)
    i = pl.program_id(2)
    n_sub = STEP_KEYS // KEY_BLOCK
    n_vreg_rows = KEY_BLOCK // SUBLANES
    n_col = 2 * tq // LANES
    qt = qt_ref[0, 0]
    row = lax.broadcasted_iota(jnp.int32, qt.shape, 0)
    zero = jnp.zeros_like(qt)
    qcat = jnp.concatenate([jnp.where(row < HEAD_DIM, qt, zero),
                            jnp.where(row >= HEAD_DIM, qt, zero)], axis=1)
    bias = [jnp.full((SUBLANES, LANES), bias_ref[2 * p + h] * LOG2E, F32) for h in range(2)]
    diag = (lax.broadcasted_iota(jnp.int32, (SUBLANES, LANES), 1)
            - RUN * lax.broadcasted_iota(jnp.int32, (SUBLANES, LANES), 0))

    def logits(kb):
        return jnp.dot(kp_ref[0, 0, kb], qcat, preferred_element_type=F32)

    def weights(carry, masked):
        new_carry = []
        for c in range(n_col):
            lanes = slice(c * LANES, (c + 1) * LANES)
            q_off = (c * LANES) % tq
            tot = carry[c]
            for sub in range(n_sub - 1, -1, -1):
                t_part = [None] * n_vreg_rows
                mask = [None] * n_vreg_rows
                run = jnp.zeros((SUBLANES, LANES), F32)
                for ii in range(n_vreg_rows - 1, -1, -1):
                    r0 = sub * KEY_BLOCK + ii * SUBLANES
                    y = y_buf[r0:r0 + SUBLANES, lanes] - bias[c * LANES // tq]
                    l1m = _log2_sigmoid(y)
                    if masked:
                        mask[ii] = (sub * KEY_BLOCK + ii - q_off) < diag
                        l1m = jnp.where(mask[ii], l1m, 0.0)
                    run = run + l1m
                    t_part[ii] = run - y
                incl = _suffix_sum_sublanes(run)
                off = incl - run + tot
                tot = tot + jnp.broadcast_to(incl[0:1], incl.shape)
                for ii in range(0, n_vreg_rows, 2):
                    pair = []
                    for jj in (ii, ii + 1):
                        a = jnp.exp2(t_part[jj] + off)
                        if masked:
                            a = jnp.where(mask[jj], a, 0.0)
                        pair.append(a)
                    r0 = sub * KEY_BLOCK + ii * SUBLANES
                    a_buf[r0:r0 + 2 * SUBLANES, lanes] = jnp.concatenate(pair, axis=0).astype(BF16)
            new_carry.append(tot)
        return tuple(new_carry)

    def accumulate(kb):
        vt = vt_ref[0, 0, kb]
        acc_ref[0] += jnp.dot(vt[:HEAD_DIM], a_buf[:, :tq], preferred_element_type=F32)
        acc_ref[1] += jnp.dot(vt[HEAD_DIM:], a_buf[:, tq:], preferred_element_type=F32)

    y_buf[...] = logits(i)
    acc_ref[...] = jnp.zeros_like(acc_ref)
    y_next = logits(jnp.maximum(i - 1, 0))
    carry = weights((jnp.zeros((SUBLANES, LANES), F32),) * n_col, True)
    y_buf[...] = y_next

    def live(carry):
        top = functools.reduce(jnp.maximum, carry)
        return jnp.max(top) > UNDERFLOW_LOG2

    def cond(state):
        t, alive, _ = state
        return jnp.logical_and(t <= i, alive)

    def body(state):
        t, _, carry = state
        alive = live(carry)
        y_next = logits(jnp.maximum(i - t - 1, 0))
        accumulate(i - t + 1)
        carry = weights(carry, False)
        y_buf[...] = y_next
        return t + 1, alive, carry

    t_end, _, _ = lax.while_loop(cond, body, (jnp.int32(1), live(carry), carry))
    accumulate(i - t_end + 1)
    o_ref[0] = jnp.concatenate([_rms_scale(acc_ref[0], axis=0),
                                _rms_scale(acc_ref[1], axis=0)], axis=0).T


def _prompt_attention(q, k, v, bias, tq):
    assert tq == STEP_KEYS
    nb, t, _ = q.shape
    nkb = t // STEP_KEYS
    n_sub = STEP_KEYS // KEY_BLOCK
    n_pairs = D_SB // LANES
    qt = q.reshape(nb, t, n_pairs, LANES).transpose(0, 2, 3, 1)
    kb_ = k.astype(BF16).reshape(nb, nkb, n_sub, SUBLANES, RUN, n_pairs, LANES)
    kp = kb_.transpose(0, 5, 1, 2, 4, 3, 6).reshape(nb, n_pairs, nkb, STEP_KEYS, LANES)
    vb_ = v.astype(BF16).reshape(nb, nkb, n_sub, SUBLANES, RUN, n_pairs, LANES)
    vt = vb_.transpose(0, 5, 1, 6, 2, 4, 3).reshape(nb, n_pairs, nkb, LANES, STEP_KEYS)
    k_spec = pl.BlockSpec((1, 1, nkb, STEP_KEYS, LANES), lambda b, p, i: (b, p, 0, 0, 0))
    v_spec = pl.BlockSpec((1, 1, nkb, LANES, STEP_KEYS), lambda b, p, i: (b, p, 0, 0, 0))
    return pl.pallas_call(
        functools.partial(_pattn_kernel, tq),
        grid=(nb, n_pairs, t // tq),
        in_specs=[pl.BlockSpec(memory_space=pltpu.SMEM),
                  pl.BlockSpec((1, 1, LANES, tq), lambda b, p, i: (b, p, 0, i)),
                  k_spec, v_spec],
        out_specs=pl.BlockSpec((1, tq, LANES), lambda b, p, i: (b, i, p)),
        out_shape=jax.ShapeDtypeStruct((nb, t, D_SB), F32),
        scratch_shapes=[pltpu.VMEM((STEP_KEYS, 2 * tq), F32),
                        pltpu.VMEM((STEP_KEYS, 2 * tq), BF16),
                        pltpu.VMEM((2, HEAD_DIM, tq), F32)],
        compiler_params=pltpu.CompilerParams(
            dimension_semantics=("arbitrary", "arbitrary", "arbitrary"),
            vmem_limit_bytes=VMEM_LIMIT),
        name="prompt_attn",
    )(bias, qt, kp, vt)


def _sattn_kernel(n_pg, t_new, pt_ref, bias_ref, q_ref, kn_ref, vn_ref, tri_ref, *refs):
    k_refs = refs[:n_pg]
    v_refs = refs[n_pg:2 * n_pg]
    o_ref, carry_ref, acc_ref = refs[2 * n_pg:]
    j = pl.program_id(1)
    q = q_ref[0]
    bias = bias_ref[...] * LOG2E
    tri = tri_ref[...]
    nt = (((1,), (1,)), ((), ()))

    def blocks(kts, vts, masked):
        ys = [jnp.dot(q, kt, preferred_element_type=F32) - bias for kt in kts]
        l1ms = [_log2_sigmoid(y) for y in ys]
        lss = [l1m - y for l1m, y in zip(l1ms, ys)]
        if masked:
            col = lax.broadcasted_iota(jnp.int32, ys[0].shape, 1)
            t_of_row = lax.broadcasted_iota(jnp.int32, ys[0].shape, 0) % t_new
            mask = col < t_of_row
            l1ms = [jnp.where(mask, l1m, 0.0) for l1m in l1ms]
        his = [l1m.astype(BF16) for l1m in l1ms]
        los = [(l1m - hi.astype(F32)).astype(BF16) for l1m, hi in zip(l1ms, his)]
        afters = [jnp.dot(hi, tri, preferred_element_type=F32)
                  + jnp.dot(lo, tri, preferred_element_type=F32) for hi, lo in zip(his, los)]
        sums = [jnp.sum(l1m, axis=1, keepdims=True) for l1m in l1ms]
        carry = carry_ref[...]
        pv = None
        for ls, after, tot, vt in zip(lss, afters, sums, vts):
            a = jnp.exp2(ls + after + carry)
            if masked:
                a = jnp.where(mask, a, 0.0)
            carry = carry + tot
            contrib = lax.dot_general(a, vt, nt, preferred_element_type=F32)
            pv = contrib if pv is None else pv + contrib
        carry_ref[...] = carry
        return pv

    @pl.when(j == 0)
    def _():
        carry_ref[...] = jnp.zeros_like(carry_ref)
        acc_ref[...] = blocks([kn_ref[0].reshape(D_SB, PAGE)], [vn_ref[0].reshape(D_SB, PAGE)], True)

    acc_ref[...] += blocks([r[0, 0].reshape(D_SB, PAGE) for r in k_refs],
                           [r[0, 0].reshape(D_SB, PAGE) for r in v_refs], False)

    @pl.when(j == pl.num_programs(1) - 1)
    def _():
        for h in range(N_HEADS):
            blk = acc_ref[h * t_new:(h + 1) * t_new, h * HEAD_DIM:(h + 1) * HEAD_DIM]
            o_ref[0, :, h * HEAD_DIM:(h + 1) * HEAD_DIM] = _rms_scale(blk, axis=1)


def _sample_attention(q, k_new, v_new, bias, cache_kt, cache_vt, page_table, layer, n_pg):
    ns, t_new, _ = q.shape
    n_pages = page_table.shape[1]
    n_rows = N_HEADS * t_new
    qh = q.astype(F32).reshape(ns, t_new, N_HEADS, HEAD_DIM).transpose(0, 2, 1, 3)
    eye = jnp.eye(N_HEADS, dtype=F32)
    qbd = (qh[:, :, :, None, :] * eye[None, :, None, :, None]).reshape(ns, n_rows, D_SB)
    bias_col = jnp.repeat(bias, t_new).reshape(n_rows, 1)
    pad = ((0, 0), (0, 0), (0, 0), (0, PAGE - t_new))
    kn = jnp.pad(k_new.reshape(ns, t_new, N_HEADS, HEAD_DIM).transpose(0, 2, 3, 1), pad)
    vn = jnp.pad(v_new.reshape(ns, t_new, N_HEADS, HEAD_DIM).transpose(0, 2, 3, 1), pad)
    jj = lax.broadcasted_iota(jnp.int32, (PAGE, PAGE), 0)
    ss = lax.broadcasted_iota(jnp.int32, (PAGE, PAGE), 1)
    tri = (jj > ss).astype(BF16)

    def page_spec(i):
        return pl.BlockSpec(
            (1, 1, N_HEADS, HEAD_DIM, PAGE),
            lambda b, j, pt_: (layer, pt_[b, n_pages - 1 - (j * n_pg + i)], 0, 0, 0))

    per_seq = lambda shape: pl.BlockSpec((1,) + shape, lambda b, j, pt_: (b,) + (0,) * len(shape))
    grid_spec = pltpu.PrefetchScalarGridSpec(
        num_scalar_prefetch=1,
        grid=(ns, n_pages // n_pg),
        in_specs=[pl.BlockSpec((n_rows, 1), lambda b, j, pt_: (0, 0)),
                  per_seq((n_rows, D_SB)),
                  per_seq((N_HEADS, HEAD_DIM, PAGE)), per_seq((N_HEADS, HEAD_DIM, PAGE)),
                  pl.BlockSpec((PAGE, PAGE), lambda b, j, pt_: (0, 0))]
                 + [page_spec(i) for i in range(n_pg)] * 2,
        out_specs=per_seq((t_new, D_SB)),
        scratch_shapes=[pltpu.VMEM((n_rows, PAGE), F32), pltpu.VMEM((n_rows, D_SB), F32)],
    )
    return pl.pallas_call(
        functools.partial(_sattn_kernel, n_pg, t_new),
        grid_spec=grid_spec,
        out_shape=jax.ShapeDtypeStruct((ns, t_new, D_SB), F32),
        compiler_params=pltpu.CompilerParams(dimension_semantics=("arbitrary", "arbitrary"),
                                             vmem_limit_bytes=VMEM_LIMIT),
        name="sample_attn",
    )(page_table, bias_col, qbd, kn, vn, tri, *([cache_kt] * n_pg), *([cache_vt] * n_pg))


def _conv_residues(stride, halo_rows):
    offs = [halo_rows - (CONV_W - 1 - j) * stride for j in range(CONV_W)]
    return sorted({off % SUBLANES for off in offs} - {0})


def _mix_kernel(stride, halo_rows, zero_first_halo, alpha,
                x_ref, u_ref, halo_ref, attn_ref, g1_ref, wdw_ref, bdw_ref, clg_ref, clb_ref,
                gsb_ref, gcv_ref, wout_ref, lg_ref, lb_ref, o_ref, xp_ref, xs_ref):
    tm = u_ref.shape[1]
    total = halo_rows + tm
    halo = halo_ref[0]
    if zero_first_halo:
        halo = jnp.where(pl.program_id(1) == 0, 0.0, halo)
    xp_ref[0:halo_rows] = halo
    xp_ref[halo_rows:total] = u_ref[0]
    offsets = [halo_rows - (CONV_W - 1 - j) * stride for j in range(CONV_W)]
    residues = _conv_residues(stride, halo_rows)
    for idx, r in enumerate(residues):
        xs_ref[idx, 0:total - SUBLANES] = xp_ref[pl.ds(r, total - SUBLANES), :]
    cv = jnp.zeros((tm, u_ref.shape[2]), F32) + bdw_ref[...]
    for j, off in enumerate(offsets):
        r = off % SUBLANES
        if r == 0:
            tap = xp_ref[pl.ds(off, tm), :]
        else:
            tap = xs_ref[residues.index(r), pl.ds(off - r, tm), :]
        cv = cv + wdw_ref[j:j + 1, :] * tap
    cv = _layernorm(cv, clg_ref[...], clb_ref[...])
    cv = cv * jax.nn.sigmoid(cv)
    cv = _rms_scale(cv, axis=-1) * gcv_ref[...]
    attn = attn_ref[0] * gsb_ref[...]
    d_sb = attn.shape[1]
    mixed = (jnp.dot(attn.astype(BF16), wout_ref[0:d_sb, :], preferred_element_type=F32)
             + jnp.dot(cv.astype(BF16), wout_ref[d_sb:, :], preferred_element_type=F32))
    y = alpha * x_ref[0] + (1.0 + g1_ref[0]) * mixed
    o_ref[0] = _layernorm(y, lg_ref[...], lb_ref[...])


def _mix(x, u, halo, halo_spec, attn, g1, params, tm, stride, halo_rows, zero_first_halo, alpha):
    nb, t, d = x.shape
    d_conv = u.shape[2]
    row = lambda w: pl.BlockSpec((1, tm, w), lambda b, i: (b, i, 0))
    return pl.pallas_call(
        functools.partial(_mix_kernel, stride, halo_rows, zero_first_halo, alpha),
        grid=(nb, t // tm),
        in_specs=[row(d), row(d_conv), halo_spec, row(D_SB), _mod_spec(g1)]
                 + [_const_spec(a) for a in params],
        out_specs=row(d),
        out_shape=jax.ShapeDtypeStruct((nb, t, d), F32),
        scratch_shapes=[pltpu.VMEM((halo_rows + tm, d_conv), F32),
                        pltpu.VMEM((max(len(_conv_residues(stride, halo_rows)), 1),
                                    halo_rows + tm, d_conv), F32)],
        compiler_params=pltpu.CompilerParams(dimension_semantics=("arbitrary", "arbitrary"),
                                             vmem_limit_bytes=VMEM_LIMIT),
        name="mix",
    )(x, u, halo, attn, g1, *params)


FFN_CHUNK = 256


def _ffn_kernel(stride, halo_rows, alpha,
                x_ref, sh_ref, sc_ref, g2_ref, halo_ref, wup_ref, wf_ref, bf_ref, wdn_ref,
                lg_ref, lb_ref, o_ref, st_ref, xp_ref, act_ref):
    tm = x_ref.shape[1]
    d_ff = wdn_ref.shape[0]

    @pl.when(pl.program_id(1) == 0)
    def _():
        xp_ref[0:halo_rows, :] = halo_ref[0]

    x = x_ref[0]
    h2 = (x * (1.0 + sc_ref[0]) + sh_ref[0]).astype(BF16)
    for c in range(d_ff // FFN_CHUNK):
        lo = c * FFN_CHUNK
        hi = lo + FFN_CHUNK
        val = jnp.dot(h2, wup_ref[:, lo:hi], preferred_element_type=F32)
        gt = jnp.dot(h2, wup_ref[:, d_ff + lo:d_ff + hi], preferred_element_type=F32)
        xp_ref[halo_rows:halo_rows + tm, lo:hi] = gt
        conv = (wf_ref[0:1, lo:hi] * xp_ref[pl.ds(halo_rows - 2 * stride, tm), lo:hi]
                + wf_ref[1:2, lo:hi] * xp_ref[pl.ds(halo_rows - stride, tm), lo:hi]
                + wf_ref[2:3, lo:hi] * gt + bf_ref[:, lo:hi])
        act = 0.5 * conv * (1.0 + lax.erf(conv * (2.0 ** -0.5))) * val
        act_ref[:, lo:hi] = act.astype(BF16)
    f = jnp.dot(act_ref[...], wdn_ref[...], preferred_element_type=F32)
    y = alpha * x + (1.0 + g2_ref[0]) * f
    o_ref[0] = _layernorm(y, lg_ref[...], lb_ref[...])
    last = xp_ref[tm:tm + halo_rows, :]
    st_ref[0] = last
    xp_ref[0:halo_rows, :] = last


def _ffn(x, sh, sc, g2, halo0, params, tm, stride, halo_rows, alpha):
    nb, t, d = x.shape
    d_ff = halo0.shape[2]
    row = pl.BlockSpec((1, tm, d), lambda b, i: (b, i, 0))
    st_spec = pl.BlockSpec((1, halo_rows, d_ff), lambda b, i: (b, 0, 0))
    return pl.pallas_call(
        functools.partial(_ffn_kernel, stride, halo_rows, alpha),
        grid=(nb, t // tm),
        in_specs=[row, _mod_spec(sh), _mod_spec(sc), _mod_spec(g2), st_spec]
                 + [_const_spec(a) for a in params],
        out_specs=[row, st_spec],
        out_shape=[jax.ShapeDtypeStruct((nb, t, d), F32),
                   jax.ShapeDtypeStruct((nb, halo_rows, d_ff), F32)],
        scratch_shapes=[pltpu.VMEM((halo_rows + tm, d_ff), F32),
                        pltpu.VMEM((tm, d_ff), BF16)],
        compiler_params=pltpu.CompilerParams(dimension_semantics=("arbitrary", "arbitrary"),
                                             vmem_limit_bytes=VMEM_LIMIT),
        name="ffn",
    )(x, sh, sc, g2, halo0, *params)


PROMPT_TM = 512
PROMPT_TQ = 256
PROMPT_HALO = 32
SAMPLE_PAGES_PER_STEP = 16


def _time_major(a):
    s, t, c = a.shape
    return a.transpose(1, 0, 2).reshape(1, t * s, c)


def _seq_major(a, s):
    _, ts, c = a.shape
    return a.reshape(ts // s, s, c).transpose(1, 0, 2)


@jax.jit
def kernel(x_prompt, x_sample, cache_k, cache_v, state_conv, state_ffn, page_table, c_prompt, c_sample, w_ada, b_ada, w_in, sb_bias, w_dw, b_dw, cln_g, cln_b, gn_sb, gn_conv, w_out, ln1_g, ln1_b, w_up, w_fdw, b_fdw, w_down, ln2_g, ln2_b):
    depth = w_in.shape[0]
    alpha = (2.0 * depth) ** 0.25
    nb, t, d = x_prompt.shape
    ns, tn, _ = x_sample.shape
    d_ff = w_down.shape[1]
    d_conv = w_dw.shape[2]
    cache_kt = jnp.transpose(cache_k, (0, 1, 3, 4, 2))
    cache_vt = jnp.transpose(cache_v, (0, 1, 3, 4, 2))
    vec = lambda a: a.reshape(1, -1)

    xp = x_prompt
    xs = _time_major(x_sample)
    outs = [[] for _ in range(8)]
    for layer in range(depth):
        w_in_bf = w_in[layer].astype(BF16)
        w_out_bf = w_out[layer].astype(BF16)
        w_up_bf = w_up[layer].astype(BF16)
        w_down_bf = w_down[layer].astype(BF16)
        mix_params = (w_dw[layer], vec(b_dw[layer]), vec(cln_g[layer]), vec(cln_b[layer]),
                      vec(gn_sb[layer]), vec(gn_conv[layer]), w_out_bf,
                      vec(ln1_g[layer]), vec(ln1_b[layer]))
        ffn_params = (w_up_bf, w_fdw[layer], vec(b_fdw[layer]), w_down_bf,
                      vec(ln2_g[layer]), vec(ln2_b[layer]))

        mod = _ada(jnp.concatenate([c_prompt, c_sample], axis=0), w_ada[layer], b_ada[layer])
        mod_p = [m.reshape(nb, 1, d) for m in jnp.split(mod[:nb], 6, axis=-1)]
        mod_s = [jnp.tile(m, (tn, 1)).reshape(1, tn * ns, d)
                 for m in jnp.split(mod[nb:], 6, axis=-1)]

        q, k, v, u = _inproj(xp, mod_p[0], mod_p[1], w_in_bf, PROMPT_TM)
        attn = _prompt_attention(q, k, v, sb_bias[layer], PROMPT_TQ)
        halo_blocks = PROMPT_TM // PROMPT_HALO
        halo_spec = pl.BlockSpec((1, PROMPT_HALO, d_conv),
                                 lambda b, i: (b, jnp.maximum(i * halo_blocks - 1, 0), 0))
        x1 = _mix(xp, u, u, halo_spec, attn, mod_p[2], mix_params, PROMPT_TM, 1, PROMPT_HALO,
                  True, alpha)
        xp, fst = _ffn(x1, mod_p[3], mod_p[4], mod_p[5], jnp.zeros((nb, SUBLANES, d_ff), F32),
                       ffn_params, PROMPT_TM, 1, SUBLANES, alpha)
        outs[0].append(k.reshape(nb, t, N_HEADS, HEAD_DIM))
        outs[1].append(v.reshape(nb, t, N_HEADS, HEAD_DIM))
        outs[2].append(u[:, t - (CONV_W - 1):])
        outs[3].append(fst[:, SUBLANES - (FFN_CONV_W - 1):])

        rows = tn * ns
        q, k, v, u = _inproj(xs, mod_s[0], mod_s[1], w_in_bf, rows)
        k_sm = _seq_major(k, ns)
        v_sm = _seq_major(v, ns)
        attn = _sample_attention(_seq_major(q, ns), k_sm, v_sm, sb_bias[layer], cache_kt,
                                 cache_vt, page_table, layer, SAMPLE_PAGES_PER_STEP)
        conv_prev = _time_major(state_conv[layer])
        conv_rows = (CONV_W - 1) * ns
        halo_spec = pl.BlockSpec((1, conv_rows, d_conv), lambda b, i: (0, 0, 0))
        x1 = _mix(xs, u, conv_prev, halo_spec, _time_major(attn), mod_s[2], mix_params, rows, ns,
                  conv_rows, False, alpha)
        ffn_rows = (FFN_CONV_W - 1) * ns
        xs, fst = _ffn(x1, mod_s[3], mod_s[4], mod_s[5], _time_major(state_ffn[layer]),
                       ffn_params, rows, ns, ffn_rows, alpha)
        outs[4].append(k_sm.reshape(ns, tn, N_HEADS, HEAD_DIM))
        outs[5].append(v_sm.reshape(ns, tn, N_HEADS, HEAD_DIM))
        conv_all = jnp.concatenate([conv_prev, u], axis=1)
        outs[6].append(_seq_major(conv_all[:, conv_all.shape[1] - conv_rows:], ns))
        outs[7].append(_seq_major(fst, ns))

    y_sample = _seq_major(xs, ns)
    k_p, v_p, c_p, f_p, k_s, v_s, c_s, f_s = [jnp.stack(o) for o in outs]
    return (xp, y_sample, k_p, v_p, c_p, f_p, k_s, v_s, c_s, f_s)
```

```python
import functools

import jax
import jax.numpy as jnp
from jax import lax
from jax.experimental import pallas as pl
from jax.experimental.pallas import tpu as pltpu

N_HEADS = 8
HEAD_DIM = 64
D_SB = N_HEADS * HEAD_DIM
CONV_W = 31
FFN_CONV_W = 3
PAGE = 128
LN_EPS = 1e-5

LANES = 128
SUBLANES = 8
KEY_BLOCK = LANES
RUN = KEY_BLOCK // SUBLANES
STEP_KEYS = 2 * KEY_BLOCK
VMEM_LIMIT = 56 * 1024 * 1024

F32 = jnp.float32
BF16 = jnp.bfloat16


def _layernorm(x, g, b):
    mu = jnp.mean(x, axis=-1, keepdims=True)
    xc = x - mu
    var = jnp.mean(xc * xc, axis=-1, keepdims=True)
    return xc * lax.rsqrt(var + LN_EPS) * g + b


def _rms_scale(x, axis):
    ms = jnp.mean(x * x, axis=axis, keepdims=True)
    return x * lax.rsqrt(ms + LN_EPS)


LOG2E = 1.4426950408889634
Q_SCALE = -(HEAD_DIM ** -0.5) * LOG2E
UNDERFLOW_LOG2 = -160.0


def _log2_sigmoid(y):
    return jnp.minimum(y, 0.0) - jnp.log2(1.0 + jnp.exp2(-jnp.abs(y)))


def _ada_kernel(c_ref, w_ref, b_ref, o_ref):
    c = c_ref[...]
    s = c * jax.nn.sigmoid(c)
    o_ref[...] = jnp.dot(s, w_ref[...], preferred_element_type=F32) + b_ref[...]


def _ada(c, w, b):
    n, d = c.shape
    n_out = w.shape[1]
    tn = 1536
    return pl.pallas_call(
        _ada_kernel,
        grid=(n_out // tn,),
        in_specs=[pl.BlockSpec((n, d), lambda j: (0, 0)),
                  pl.BlockSpec((d, tn), lambda j: (0, j)),
                  pl.BlockSpec((1, tn), lambda j: (0, j))],
        out_specs=pl.BlockSpec((n, tn), lambda j: (0, j)),
        out_shape=jax.ShapeDtypeStruct((n, n_out), F32),
        compiler_params=pltpu.CompilerParams(dimension_semantics=("arbitrary",),
                                             vmem_limit_bytes=VMEM_LIMIT),
        name="ada",
    )(c, w, b.reshape(1, n_out))


def _inproj_kernel(q_scale, permuted_kv, x_ref, sh_ref, sc_ref, w_ref, perm_ref, qt_ref, *out_refs):
    if permuted_kv:
        kp_ref, vt_ref, kt_ref, vtn_ref, u_ref = out_refs
    else:
        kt_ref, vtn_ref, u_ref = out_refs
    tm = x_ref.shape[1]
    n_pairs = D_SB // LANES
    h = x_ref[0] * (1.0 + sc_ref[0]) + sh_ref[0]
    proj = jnp.dot(h.astype(BF16), w_ref[...], preferred_element_type=F32)
    q = proj[:, :D_SB] * q_scale
    k = proj[:, D_SB:2 * D_SB]
    v = proj[:, 2 * D_SB:3 * D_SB]
    qt_ref[0] = q.T.astype(BF16).reshape(n_pairs, LANES, tm)
    kt_ref[0] = k.T
    vtn_ref[0] = v.T
    perm = perm_ref[...]
    for j in range(tm // KEY_BLOCK if permuted_kv else 0):
        rows = slice(j * KEY_BLOCK, (j + 1) * KEY_BLOCK)
        step, half = divmod(j, STEP_KEYS // KEY_BLOCK)
        keys = slice(half * KEY_BLOCK, (half + 1) * KEY_BLOCK)
        kpj = jnp.dot(perm, k[rows].astype(BF16), preferred_element_type=F32).astype(BF16)
        vpj = jnp.dot(perm, v[rows].astype(BF16), preferred_element_type=F32).T.astype(BF16)
        for p in range(n_pairs):
            dims = slice(p * LANES, (p + 1) * LANES)
            kp_ref[0, p, step, keys, :] = kpj[:, dims]
            vt_ref[0, p, step, :, keys] = vpj[dims, :]
    d_conv = (proj.shape[1] - 3 * D_SB) // 2
    a = proj[:, 3 * D_SB:3 * D_SB + d_conv]
    g = proj[:, 3 * D_SB + d_conv:]
    u_ref[0] = a * jax.nn.sigmoid(g)


def _mod_spec(m):
    return pl.BlockSpec((1,) + m.shape[1:], lambda b, i: (b, 0, 0))


def _const_spec(a):
    nd = a.ndim
    return pl.BlockSpec(a.shape, lambda b, i: (0,) * nd, pipeline_mode=pl.Buffered(1))


def _key_permutation():
    r = lax.broadcasted_iota(jnp.int32, (KEY_BLOCK, KEY_BLOCK), 0)
    c = lax.broadcasted_iota(jnp.int32, (KEY_BLOCK, KEY_BLOCK), 1)
    return (c == (r % SUBLANES) * RUN + r // SUBLANES).astype(BF16)


def _inproj(x, sh, sc, w_in_bf, tm, permuted_kv):
    nb, t, d = x.shape
    d_conv = (w_in_bf.shape[1] - 3 * D_SB) // 2
    n_pairs = D_SB // LANES
    steps = tm // STEP_KEYS
    perm = _key_permutation()
    row = lambda w: pl.BlockSpec((1, tm, w), lambda b, i: (b, i, 0))
    col = pl.BlockSpec((1, D_SB, tm), lambda b, i: (b, 0, i))
    kv_specs, kv_shapes = [], []
    if permuted_kv:
        kv_specs = [pl.BlockSpec((1, n_pairs, steps, STEP_KEYS, LANES), lambda b, i: (b, 0, i, 0, 0)),
                    pl.BlockSpec((1, n_pairs, steps, LANES, STEP_KEYS), lambda b, i: (b, 0, i, 0, 0))]
        kv_shapes = [jax.ShapeDtypeStruct((nb, n_pairs, t // STEP_KEYS, STEP_KEYS, LANES), BF16),
                     jax.ShapeDtypeStruct((nb, n_pairs, t // STEP_KEYS, LANES, STEP_KEYS), BF16)]
    return pl.pallas_call(
        functools.partial(_inproj_kernel, Q_SCALE, permuted_kv),
        grid=(nb, t // tm),
        in_specs=[row(d), _mod_spec(sh), _mod_spec(sc), _const_spec(w_in_bf), _const_spec(perm)],
        out_specs=[pl.BlockSpec((1, n_pairs, LANES, tm), lambda b, i: (b, 0, 0, i))] + kv_specs
                  + [col, col, row(d_conv)],
        out_shape=[jax.ShapeDtypeStruct((nb, n_pairs, LANES, t), BF16)] + kv_shapes
                  + [jax.ShapeDtypeStruct((nb, D_SB, t), F32),
                     jax.ShapeDtypeStruct((nb, D_SB, t), F32),
                     jax.ShapeDtypeStruct((nb, t, d_conv), F32)],
        compiler_params=pltpu.CompilerParams(dimension_semantics=("arbitrary", "arbitrary"),
                                             vmem_limit_bytes=VMEM_LIMIT),
        name="inproj",
    )(x, sh, sc, w_in_bf, perm)


def _suffix_sum_sublanes(x):
    s = lax.broadcasted_iota(jnp.int32, x.shape, 0)
    for sh in (1, 2, 4):
        x = x + jnp.where(s + sh < SUBLANES, pltpu.roll(x, SUBLANES - sh, axis=0), 0.0)
    return x


def _pattn_kernel(tq, bias_ref, qt_ref, kp_ref, vt_ref, o_ref, y_buf, a_buf, acc_ref):
    p = pl.program_id(1)
    i = pl.program_id(2)
    n_sub = STEP_KEYS // KEY_BLOCK
    n_vreg_rows = KEY_BLOCK // SUBLANES
    n_col = 2 * tq // LANES
    qt = qt_ref[0, 0]
    row = lax.broadcasted_iota(jnp.int32, qt.shape, 0)
    zero = jnp.zeros_like(qt)
    qcat = jnp.concatenate([jnp.where(row < HEAD_DIM, qt, zero),
                            jnp.where(row >= HEAD_DIM, qt, zero)], axis=1)
    bias = [jnp.full((SUBLANES, LANES), bias_ref[2 * p + h] * LOG2E, F32) for h in range(2)]
    diag = (lax.broadcasted_iota(jnp.int32, (SUBLANES, LANES), 1)
            - RUN * lax.broadcasted_iota(jnp.int32, (SUBLANES, LANES), 0))

    def logits(kb):
        return jnp.dot(kp_ref[0, 0, kb], qcat, preferred_element_type=F32)

    def weights(carry, masked):
        new_carry = []
        for c in range(n_col):
            lanes = slice(c * LANES, (c + 1) * LANES)
            q_off = (c * LANES) % tq
            tot = carry[c]
            for sub in range(n_sub - 1, -1, -1):
                key0 = sub * KEY_BLOCK
                if masked and key0 >= q_off + LANES - 1:
                    a_buf[key0:key0 + KEY_BLOCK, lanes] = jnp.zeros((KEY_BLOCK, LANES), BF16)
                    continue
                use_mask = masked and key0 + KEY_BLOCK > q_off
                t_part = [None] * n_vreg_rows
                mask = [None] * n_vreg_rows
                run = jnp.zeros((SUBLANES, LANES), F32)
                for ii in range(n_vreg_rows - 1, -1, -1):
                    r0 = sub * KEY_BLOCK + ii * SUBLANES
                    y = y_buf[r0:r0 + SUBLANES, lanes] - bias[c * LANES // tq]
                    l1m = _log2_sigmoid(y)
                    if use_mask:
                        mask[ii] = (sub * KEY_BLOCK + ii - q_off) < diag
                        l1m = jnp.where(mask[ii], l1m, 0.0)
                    run = run + l1m
                    t_part[ii] = run - y
                incl = _suffix_sum_sublanes(run)
                off = incl - run + tot
                tot = tot + jnp.broadcast_to(incl[0:1], incl.shape)
                for ii in range(0, n_vreg_rows, 2):
                    pair = []
                    for jj in (ii, ii + 1):
                        a = jnp.exp2(t_part[jj] + off)
                        if use_mask:
                            a = jnp.where(mask[jj], a, 0.0)
                        pair.append(a)
                    r0 = sub * KEY_BLOCK + ii * SUBLANES
                    a_buf[r0:r0 + 2 * SUBLANES, lanes] = jnp.concatenate(pair, axis=0).astype(BF16)
            new_carry.append(tot)
        return tuple(new_carry)

    def accumulate(kb):
        vt = vt_ref[0, 0, kb]
        acc_ref[0] += jnp.dot(vt[:HEAD_DIM], a_buf[:, :tq], preferred_element_type=F32)
        acc_ref[1] += jnp.dot(vt[HEAD_DIM:], a_buf[:, tq:], preferred_element_type=F32)

    y_buf[...] = logits(i)
    acc_ref[...] = jnp.zeros_like(acc_ref)
    y_next = logits(jnp.maximum(i - 1, 0))
    carry = weights((jnp.zeros((SUBLANES, LANES), F32),) * n_col, True)
    y_buf[...] = y_next

    def live(carry):
        top = functools.reduce(jnp.maximum, carry)
        return jnp.max(top) > UNDERFLOW_LOG2

    def cond(state):
        t, alive, _ = state
        return jnp.logical_and(t <= i, alive)

    def body(state):
        t, _, carry = state
        alive = live(carry)
        y_next = logits(jnp.maximum(i - t - 1, 0))
        accumulate(i - t + 1)
        carry = weights(carry, False)
        y_buf[...] = y_next
        return t + 1, alive, carry

    t_end, _, _ = lax.while_loop(cond, body, (jnp.int32(1), live(carry), carry))
    accumulate(i - t_end + 1)
    o_ref[0] = jnp.concatenate([_rms_scale(acc_ref[0], axis=0),
                                _rms_scale(acc_ref[1], axis=0)], axis=0).T


def _prompt_attention(qt, kp, vt, bias, tq):
    assert tq == STEP_KEYS
    nb, n_pairs, _, t = qt.shape
    nkb = t // STEP_KEYS
    k_spec = pl.BlockSpec((1, 1, nkb, STEP_KEYS, LANES), lambda b, p, i: (b, p, 0, 0, 0))
    v_spec = pl.BlockSpec((1, 1, nkb, LANES, STEP_KEYS), lambda b, p, i: (b, p, 0, 0, 0))
    return pl.pallas_call(
        functools.partial(_pattn_kernel, tq),
        grid=(nb, n_pairs, t // tq),
        in_specs=[pl.BlockSpec(memory_space=pltpu.SMEM),
                  pl.BlockSpec((1, 1, LANES, tq), lambda b, p, i: (b, p, 0, i)),
                  k_spec, v_spec],
        out_specs=pl.BlockSpec((1, tq, LANES), lambda b, p, i: (b, i, p)),
        out_shape=jax.ShapeDtypeStruct((nb, t, D_SB), F32),
        scratch_shapes=[pltpu.VMEM((STEP_KEYS, 2 * tq), F32),
                        pltpu.VMEM((STEP_KEYS, 2 * tq), BF16),
                        pltpu.VMEM((2, HEAD_DIM, tq), F32)],
        compiler_params=pltpu.CompilerParams(
            dimension_semantics=("arbitrary", "arbitrary", "arbitrary"),
            vmem_limit_bytes=VMEM_LIMIT),
        name="prompt_attn",
    )(bias, qt, kp, vt)


def _sattn_kernel(n_pg, t_new, pt_ref, bias_ref, q_ref, kn_ref, vn_ref, tri_ref, *refs):
    k_refs = refs[:n_pg]
    v_refs = refs[n_pg:2 * n_pg]
    o_ref, carry_ref, acc_ref = refs[2 * n_pg:]
    j = pl.program_id(1)
    q = q_ref[0]
    bias = bias_ref[...] * LOG2E
    tri = tri_ref[...]
    nt = (((1,), (1,)), ((), ()))

    def blocks(kts, vts, masked):
        ys = [jnp.dot(q, kt, preferred_element_type=F32) - bias for kt in kts]
        l1ms = [_log2_sigmoid(y) for y in ys]
        lss = [l1m - y for l1m, y in zip(l1ms, ys)]
        if masked:
            col = lax.broadcasted_iota(jnp.int32, ys[0].shape, 1)
            t_of_row = lax.broadcasted_iota(jnp.int32, ys[0].shape, 0) % t_new
            mask = col < t_of_row
            l1ms = [jnp.where(mask, l1m, 0.0) for l1m in l1ms]
        his = [l1m.astype(BF16) for l1m in l1ms]
        los = [(l1m - hi.astype(F32)).astype(BF16) for l1m, hi in zip(l1ms, his)]
        afters = [jnp.dot(hi, tri, preferred_element_type=F32)
                  + jnp.dot(lo, tri, preferred_element_type=F32) for hi, lo in zip(his, los)]
        sums = [jnp.sum(l1m, axis=1, keepdims=True) for l1m in l1ms]
        carry = carry_ref[...]
        pv = None
        for ls, after, tot, vt in zip(lss, afters, sums, vts):
            a = jnp.exp2(ls + after + carry)
            if masked:
                a = jnp.where(mask, a, 0.0)
            carry = carry + tot
            contrib = lax.dot_general(a, vt, nt, preferred_element_type=F32)
            pv = contrib if pv is None else pv + contrib
        carry_ref[...] = carry
        return pv

    @pl.when(j == 0)
    def _():
        carry_ref[...] = jnp.zeros_like(carry_ref)
        acc_ref[...] = blocks([kn_ref[0].reshape(D_SB, PAGE)], [vn_ref[0].reshape(D_SB, PAGE)], True)

    acc_ref[...] += blocks([r[0, 0].reshape(D_SB, PAGE) for r in k_refs],
                           [r[0, 0].reshape(D_SB, PAGE) for r in v_refs], False)

    @pl.when(j == pl.num_programs(1) - 1)
    def _():
        for h in range(N_HEADS):
            blk = acc_ref[h * t_new:(h + 1) * t_new, h * HEAD_DIM:(h + 1) * HEAD_DIM]
            o_ref[0, :, h * HEAD_DIM:(h + 1) * HEAD_DIM] = _rms_scale(blk, axis=1)


def _sample_attention(qh, k_new_t, v_new_t, bias, cache_kt, cache_vt, page_table, layer, n_pg):
    ns, _, t_new, _ = qh.shape
    n_pages = page_table.shape[1]
    n_rows = N_HEADS * t_new
    eye = jnp.eye(N_HEADS, dtype=F32)
    qbd = (qh.astype(F32)[:, :, :, None, :] * eye[None, :, None, :, None]).reshape(ns, n_rows, D_SB)
    bias_col = jnp.repeat(bias, t_new).reshape(n_rows, 1)
    pad = ((0, 0), (0, 0), (0, 0), (0, PAGE - t_new))
    kn = jnp.pad(k_new_t, pad)
    vn = jnp.pad(v_new_t, pad)
    jj = lax.broadcasted_iota(jnp.int32, (PAGE, PAGE), 0)
    ss = lax.broadcasted_iota(jnp.int32, (PAGE, PAGE), 1)
    tri = (jj > ss).astype(BF16)

    def page_spec(i):
        return pl.BlockSpec(
            (1, 1, N_HEADS, HEAD_DIM, PAGE),
            lambda b, j, pt_: (layer, pt_[b, n_pages - 1 - (j * n_pg + i)], 0, 0, 0))

    per_seq = lambda shape: pl.BlockSpec((1,) + shape, lambda b, j, pt_: (b,) + (0,) * len(shape))
    grid_spec = pltpu.PrefetchScalarGridSpec(
        num_scalar_prefetch=1,
        grid=(ns, n_pages // n_pg),
        in_specs=[pl.BlockSpec((n_rows, 1), lambda b, j, pt_: (0, 0)),
                  per_seq((n_rows, D_SB)),
                  per_seq((N_HEADS, HEAD_DIM, PAGE)), per_seq((N_HEADS, HEAD_DIM, PAGE)),
                  pl.BlockSpec((PAGE, PAGE), lambda b, j, pt_: (0, 0))]
                 + [page_spec(i) for i in range(n_pg)] * 2,
        out_specs=per_seq((t_new, D_SB)),
        scratch_shapes=[pltpu.VMEM((n_rows, PAGE), F32), pltpu.VMEM((n_rows, D_SB), F32)],
    )
    return pl.pallas_call(
        functools.partial(_sattn_kernel, n_pg, t_new),
        grid_spec=grid_spec,
        out_shape=jax.ShapeDtypeStruct((ns, t_new, D_SB), F32),
        compiler_params=pltpu.CompilerParams(dimension_semantics=("arbitrary", "arbitrary"),
                                             vmem_limit_bytes=VMEM_LIMIT),
        name="sample_attn",
    )(page_table, bias_col, qbd, kn, vn, tri, *([cache_kt] * n_pg), *([cache_vt] * n_pg))


def _conv_residues(stride, halo_rows):
    offs = [halo_rows - (CONV_W - 1 - j) * stride for j in range(CONV_W)]
    return sorted({off % SUBLANES for off in offs} - {0})


def _mix_kernel(stride, halo_rows, zero_first_halo, alpha,
                x_ref, u_ref, halo_ref, attn_ref, g1_ref, wdw_ref, bdw_ref, clg_ref, clb_ref,
                gsb_ref, gcv_ref, wout_ref, lg_ref, lb_ref, o_ref, xp_ref, xs_ref):
    tm = u_ref.shape[1]
    total = halo_rows + tm
    halo = halo_ref[0]
    if zero_first_halo:
        halo = jnp.where(pl.program_id(1) == 0, 0.0, halo)
    xp_ref[0:halo_rows] = halo
    xp_ref[halo_rows:total] = u_ref[0]
    offsets = [halo_rows - (CONV_W - 1 - j) * stride for j in range(CONV_W)]
    residues = _conv_residues(stride, halo_rows)
    for idx, r in enumerate(residues):
        xs_ref[idx, 0:total - SUBLANES] = xp_ref[pl.ds(r, total - SUBLANES), :]
    cv = jnp.zeros((tm, u_ref.shape[2]), F32) + bdw_ref[...]
    for j, off in enumerate(offsets):
        r = off % SUBLANES
        if r == 0:
            tap = xp_ref[pl.ds(off, tm), :]
        else:
            tap = xs_ref[residues.index(r), pl.ds(off - r, tm), :]
        cv = cv + wdw_ref[j:j + 1, :] * tap
    cv = _layernorm(cv, clg_ref[...], clb_ref[...])
    cv = cv * jax.nn.sigmoid(cv)
    cv = _rms_scale(cv, axis=-1) * gcv_ref[...]
    attn = attn_ref[0] * gsb_ref[...]
    d_sb = attn.shape[1]
    mixed = (jnp.dot(attn.astype(BF16), wout_ref[0:d_sb, :], preferred_element_type=F32)
             + jnp.dot(cv.astype(BF16), wout_ref[d_sb:, :], preferred_element_type=F32))
    y = alpha * x_ref[0] + (1.0 + g1_ref[0]) * mixed
    o_ref[0] = _layernorm(y, lg_ref[...], lb_ref[...])


def _mix(x, u, halo, halo_spec, attn, g1, params, tm, stride, halo_rows, zero_first_halo, alpha):
    nb, t, d = x.shape
    d_conv = u.shape[2]
    row = lambda w: pl.BlockSpec((1, tm, w), lambda b, i: (b, i, 0))
    return pl.pallas_call(
        functools.partial(_mix_kernel, stride, halo_rows, zero_first_halo, alpha),
        grid=(nb, t // tm),
        in_specs=[row(d), row(d_conv), halo_spec, row(D_SB), _mod_spec(g1)]
                 + [_const_spec(a) for a in params],
        out_specs=row(d),
        out_shape=jax.ShapeDtypeStruct((nb, t, d), F32),
        scratch_shapes=[pltpu.VMEM((halo_rows + tm, d_conv), F32),
                        pltpu.VMEM((max(len(_conv_residues(stride, halo_rows)), 1),
                                    halo_rows + tm, d_conv), F32)],
        compiler_params=pltpu.CompilerParams(dimension_semantics=("arbitrary", "arbitrary"),
                                             vmem_limit_bytes=VMEM_LIMIT),
        name="mix",
    )(x, u, halo, attn, g1, *params)


FFN_CHUNK = 256


def _ffn_kernel(stride, halo_rows, alpha,
                x_ref, sh_ref, sc_ref, g2_ref, halo_ref, wup_ref, wf_ref, bf_ref, wdn_ref,
                lg_ref, lb_ref, o_ref, st_ref, xp_ref, act_ref):
    tm = x_ref.shape[1]
    d_ff = wdn_ref.shape[0]

    @pl.when(pl.program_id(1) == 0)
    def _():
        xp_ref[0:halo_rows, :] = halo_ref[0]

    x = x_ref[0]
    h2 = (x * (1.0 + sc_ref[0]) + sh_ref[0]).astype(BF16)
    for c in range(d_ff // FFN_CHUNK):
        lo = c * FFN_CHUNK
        hi = lo + FFN_CHUNK
        val = jnp.dot(h2, wup_ref[:, lo:hi], preferred_element_type=F32)
        gt = jnp.dot(h2, wup_ref[:, d_ff + lo:d_ff + hi], preferred_element_type=F32)
        xp_ref[halo_rows:halo_rows + tm, lo:hi] = gt
        conv = (wf_ref[0:1, lo:hi] * xp_ref[pl.ds(halo_rows - 2 * stride, tm), lo:hi]
                + wf_ref[1:2, lo:hi] * xp_ref[pl.ds(halo_rows - stride, tm), lo:hi]
                + wf_ref[2:3, lo:hi] * gt + bf_ref[:, lo:hi])
        act = 0.5 * conv * (1.0 + lax.erf(conv * (2.0 ** -0.5))) * val
        act_ref[:, lo:hi] = act.astype(BF16)
    f = jnp.dot(act_ref[...], wdn_ref[...], preferred_element_type=F32)
    y = alpha * x + (1.0 + g2_ref[0]) * f
    o_ref[0] = _layernorm(y, lg_ref[...], lb_ref[...])
    last = xp_ref[tm:tm + halo_rows, :]
    st_ref[0] = last
    xp_ref[0:halo_rows, :] = last


def _ffn(x, sh, sc, g2, halo0, params, tm, stride, halo_rows, alpha):
    nb, t, d = x.shape
    d_ff = halo0.shape[2]
    row = pl.BlockSpec((1, tm, d), lambda b, i: (b, i, 0))
    st_spec = pl.BlockSpec((1, halo_rows, d_ff), lambda b, i: (b, 0, 0))
    return pl.pallas_call(
        functools.partial(_ffn_kernel, stride, halo_rows, alpha),
        grid=(nb, t // tm),
        in_specs=[row, _mod_spec(sh), _mod_spec(sc), _mod_spec(g2), st_spec]
                 + [_const_spec(a) for a in params],
        out_specs=[row, st_spec],
        out_shape=[jax.ShapeDtypeStruct((nb, t, d), F32),
                   jax.ShapeDtypeStruct((nb, halo_rows, d_ff), F32)],
        scratch_shapes=[pltpu.VMEM((halo_rows + tm, d_ff), F32),
                        pltpu.VMEM((tm, d_ff), BF16)],
        compiler_params=pltpu.CompilerParams(dimension_semantics=("arbitrary", "arbitrary"),
                                             vmem_limit_bytes=VMEM_LIMIT),
        name="ffn",
    )(x, sh, sc, g2, halo0, *params)


PROMPT_TM = 512
PROMPT_TQ = 256
PROMPT_HALO = 32
SAMPLE_PAGES_PER_STEP = 16


def _time_major(a):
    s, t, c = a.shape
    return a.transpose(1, 0, 2).reshape(1, t * s, c)


def _seq_major(a, s):
    _, ts, c = a.shape
    return a.reshape(ts // s, s, c).transpose(1, 0, 2)


@jax.jit
def kernel(x_prompt, x_sample, cache_k, cache_v, state_conv, state_ffn, page_table, c_prompt, c_sample, w_ada, b_ada, w_in, sb_bias, w_dw, b_dw, cln_g, cln_b, gn_sb, gn_conv, w_out, ln1_g, ln1_b, w_up, w_fdw, b_fdw, w_down, ln2_g, ln2_b):
    depth = w_in.shape[0]
    alpha = (2.0 * depth) ** 0.25
    nb, t, d = x_prompt.shape
    ns, tn, _ = x_sample.shape
    d_ff = w_down.shape[1]
    d_conv = w_dw.shape[2]
    cache_kt = jnp.transpose(cache_k, (0, 1, 3, 4, 2))
    cache_vt = jnp.transpose(cache_v, (0, 1, 3, 4, 2))
    vec = lambda a: a.reshape(1, -1)

    xp = x_prompt
    xs = _time_major(x_sample)
    outs = [[] for _ in range(8)]
    for layer in range(depth):
        w_in_bf = w_in[layer].astype(BF16)
        w_out_bf = w_out[layer].astype(BF16)
        w_up_bf = w_up[layer].astype(BF16)
        w_down_bf = w_down[layer].astype(BF16)
        mix_params = (w_dw[layer], vec(b_dw[layer]), vec(cln_g[layer]), vec(cln_b[layer]),
                      vec(gn_sb[layer]), vec(gn_conv[layer]), w_out_bf,
                      vec(ln1_g[layer]), vec(ln1_b[layer]))
        ffn_params = (w_up_bf, w_fdw[layer], vec(b_fdw[layer]), w_down_bf,
                      vec(ln2_g[layer]), vec(ln2_b[layer]))

        mod = _ada(jnp.concatenate([c_prompt, c_sample], axis=0), w_ada[layer], b_ada[layer])
        mod_p = [m.reshape(nb, 1, d) for m in jnp.split(mod[:nb], 6, axis=-1)]
        mod_s = [jnp.tile(m, (tn, 1)).reshape(1, tn * ns, d)
                 for m in jnp.split(mod[nb:], 6, axis=-1)]

        qt, kp, vt, k_t, v_t, u = _inproj(xp, mod_p[0], mod_p[1], w_in_bf, PROMPT_TM, True)
        attn = _prompt_attention(qt, kp, vt, sb_bias[layer], PROMPT_TQ)
        halo_blocks = PROMPT_TM // PROMPT_HALO
        halo_spec = pl.BlockSpec((1, PROMPT_HALO, d_conv),
                                 lambda b, i: (b, jnp.maximum(i * halo_blocks - 1, 0), 0))
        x1 = _mix(xp, u, u, halo_spec, attn, mod_p[2], mix_params, PROMPT_TM, 1, PROMPT_HALO,
                  True, alpha)
        xp, fst = _ffn(x1, mod_p[3], mod_p[4], mod_p[5], jnp.zeros((nb, SUBLANES, d_ff), F32),
                       ffn_params, PROMPT_TM, 1, SUBLANES, alpha)
        outs[0].append(k_t.reshape(nb, N_HEADS, HEAD_DIM, t).transpose(0, 3, 1, 2))
        outs[1].append(v_t.reshape(nb, N_HEADS, HEAD_DIM, t).transpose(0, 3, 1, 2))
        outs[2].append(u[:, t - (CONV_W - 1):])
        outs[3].append(fst[:, SUBLANES - (FFN_CONV_W - 1):])

        rows = tn * ns
        qt, k_t, v_t, u = _inproj(xs, mod_s[0], mod_s[1], w_in_bf, rows, False)
        by_head = lambda a: a.reshape(N_HEADS, HEAD_DIM, tn, ns)
        attn = _sample_attention(by_head(qt).transpose(3, 0, 2, 1),
                                 by_head(k_t).transpose(3, 0, 1, 2), by_head(v_t).transpose(3, 0, 1, 2),
                                 sb_bias[layer], cache_kt, cache_vt, page_table, layer,
                                 SAMPLE_PAGES_PER_STEP)
        conv_prev = _time_major(state_conv[layer])
        conv_rows = (CONV_W - 1) * ns
        halo_spec = pl.BlockSpec((1, conv_rows, d_conv), lambda b, i: (0, 0, 0))
        x1 = _mix(xs, u, conv_prev, halo_spec, _time_major(attn), mod_s[2], mix_params, rows, ns,
                  conv_rows, False, alpha)
        ffn_rows = (FFN_CONV_W - 1) * ns
        xs, fst = _ffn(x1, mod_s[3], mod_s[4], mod_s[5], _time_major(state_ffn[layer]),
                       ffn_params, rows, ns, ffn_rows, alpha)
        outs[4].append(by_head(k_t).transpose(3, 2, 0, 1))
        outs[5].append(by_head(v_t).transpose(3, 2, 0, 1))
        conv_all = jnp.concatenate([conv_prev, u], axis=1)
        outs[6].append(_seq_major(conv_all[:, conv_all.shape[1] - conv_rows:], ns))
        outs[7].append(_seq_major(fst, ns))

    y_sample = _seq_major(xs, ns)
    k_p, v_p, c_p, f_p, k_s, v_s, c_s, f_s = [jnp.stack(o) for o in outs]
    return (xp, y_sample, k_p, v_p, c_p, f_p, k_s, v_s, c_s, f_s)
```
